```python
import math
import jax, jax.numpy as jnp
from jax import lax
import numpy as np

D_MODEL = 1024
BATCH = 8
SEQ = 4096
DEPTH = 4

CHUNK = 64
NORM_EPS = 1e-6
POOL_WIDTH = D_MODEL // 4
POOL_WINDOWS = (2, 4, 8, 16)
POOL_GROUPS = len(POOL_WINDOWS)
POOL_GROUP_DIM = POOL_WIDTH // POOL_GROUPS
ATTN_HEAD_DIM = 64
ATTN_HEADS = D_MODEL // 256
ATTN_V_DIM = 2 * ATTN_HEAD_DIM
ATTN_QK_WIDTH = ATTN_HEADS * 2 * ATTN_HEAD_DIM
ATTN_V_WIDTH = ATTN_HEADS * ATTN_V_DIM
ROPE_THETA = 10000.0
Q_BLOCK = 128
SSM_WIDTH = D_MODEL // 4
SSM_GROUP_DIM = 16
SSM_GROUPS = SSM_WIDTH // SSM_GROUP_DIM
SSM_STATE = 64
DT_MIN = 0.001
DT_MAX = 0.1
IN_WIDTH = POOL_WIDTH + 2 * ATTN_QK_WIDTH + ATTN_V_WIDTH + SSM_WIDTH
N_BRANCHES = 3
MOE_GROUPS = 4
EXPERTS_PER_GROUP = 8
N_EXPERTS = MOE_GROUPS * EXPERTS_PER_GROUP
TOP_K_EXPERT = 2
EXPERT_FF = 512
MOE_BLOCK = 128

kernel_name = "hybrid_chunk_causal_pool_diffattn_s5_hmoe"


def rmsnorm(x, g):
    xf = x.astype(jnp.float32)
    y = xf * lax.rsqrt(jnp.mean(xf * xf, axis=-1, keepdims=True) + NORM_EPS)
    return (y * g.astype(jnp.float32)).astype(x.dtype)


def modulate(h, shift, scale):
    return h * (1.0 + scale[:, None, :]) + shift[:, None, :]


def rope_tables(positions):
    inv = ROPE_THETA ** (-jnp.arange(0, ATTN_HEAD_DIM, 2, dtype=jnp.float32) / ATTN_HEAD_DIM)
    ang = positions.astype(jnp.float32)[..., None] * inv
    return jnp.cos(ang)[:, :, None, None, :], jnp.sin(ang)[:, :, None, None, :]


def apply_rope(t, cos, sin):
    t1, t2 = jnp.split(t, 2, axis=-1)
    return jnp.concatenate([t1 * cos - t2 * sin, t2 * cos + t1 * sin], axis=-1)


def pool_mixer(u, pool_w, pool_scale):
    B, S, _ = u.shape
    ug = u.astype(jnp.float32).reshape(B, S, POOL_GROUPS, POOL_GROUP_DIM)
    cs = jnp.cumsum(ug, axis=1)
    t = jnp.arange(1, S + 1, dtype=jnp.float32)
    outs = []
    for g, w in enumerate(POOL_WINDOWS):
        cg = cs[:, :, g]
        prev = jnp.pad(cg, ((0, 0), (w, 0), (0, 0)))[:, :S]
        cnt = jnp.minimum(t, float(w))[None, :, None]
        outs.append((cg - prev) / cnt - ug[:, :, g])
    pooled = jnp.stack(outs, axis=2)
    y = jnp.einsum('bsgc,gcd->bsgd', pooled, pool_w.astype(jnp.float32)).reshape(B, S, POOL_WIDTH)
    return (y * pool_scale.astype(jnp.float32)).astype(u.dtype)


def diff_attention(q, k, v, cos, sin, q_g, k_g, lq1, lk1, lq2, lk2, subln_g, lambda_init):
    B, S, _ = q.shape
    H, dh, E = ATTN_HEADS, ATTN_HEAD_DIM, ATTN_V_DIM
    qf = rmsnorm(q.astype(jnp.float32).reshape(B, S, H, 2, dh), q_g)
    kf = rmsnorm(k.astype(jnp.float32).reshape(B, S, H, 2, dh), k_g)
    vf = v.astype(jnp.float32).reshape(B, S, H, E)
    qf = apply_rope(qf, cos, sin) * (dh ** -0.5)
    kf = apply_rope(kf, cos, sin)
    lam = (jnp.exp(jnp.sum(lq1.astype(jnp.float32) * lk1.astype(jnp.float32)))
           - jnp.exp(jnp.sum(lq2.astype(jnp.float32) * lk2.astype(jnp.float32))) + lambda_init)
    nb = S // Q_BLOCK
    qb = jnp.moveaxis(qf.reshape(B, nb, Q_BLOCK, H, 2, dh), 1, 0)
    k_chunk = jnp.arange(S) // CHUNK

    def block(args):
        qi, i = args
        s = jnp.einsum('bqhmd,bkhmd->bhmqk', qi, kf)
        q_chunk = (i * Q_BLOCK + jnp.arange(Q_BLOCK)) // CHUNK
        mask = k_chunk[None, :] <= q_chunk[:, None]
        p = jax.nn.softmax(jnp.where(mask, s, -1e30), axis=-1)
        a = p[:, :, 0] - lam * p[:, :, 1]
        return jnp.einsum('bhqk,bkhe->bqhe', a, vf)

    o = lax.map(block, (qb, jnp.arange(nb)))
    o = jnp.moveaxis(o, 0, 1).reshape(B, S, H, E)
    o = rmsnorm(o, subln_g) * (1.0 - lambda_init)
    return o.reshape(B, S, H * E).astype(q.dtype)


def ssm_mixer(u, lam_re, lam_im, log_dt, b_re, b_im, c_re, c_im, d_skip, glu_w, glu_b):
    B, S, _ = u.shape
    f32 = jnp.float32
    uf = u.astype(f32)
    ug = uf.reshape(B, S, SSM_GROUPS, SSM_GROUP_DIM)
    lr, li = lam_re.astype(f32), lam_im.astype(f32)
    dt = jnp.exp(log_dt.astype(f32))[:, None]
    mag = jnp.exp(lr * dt)
    ar, ai = mag * jnp.cos(li * dt), mag * jnp.sin(li * dt)
    nr, ni = ar - 1.0, ai
    den = lr * lr + li * li
    kr, ki = (nr * lr + ni * li) / den, (ni * lr - nr * li) / den
    br, bi = b_re.astype(f32), b_im.astype(f32)
    bbr = kr[..., None] * br - ki[..., None] * bi
    bbi = kr[..., None] * bi + ki[..., None] * br
    bur = jnp.einsum('bsgc,gpc->bsgp', ug, bbr)
    bui = jnp.einsum('bsgc,gpc->bsgp', ug, bbi)
    a_r = jnp.broadcast_to(ar, bur.shape)
    a_i = jnp.broadcast_to(ai, bui.shape)

    def combine(e1, e2):
        a1r, a1i, b1r, b1i = e1
        a2r, a2i, b2r, b2i = e2
        return (a2r * a1r - a2i * a1i, a2r * a1i + a2i * a1r,
                a2r * b1r - a2i * b1i + b2r, a2r * b1i + a2i * b1r + b2i)

    _, _, xr, xi = lax.associative_scan(combine, (a_r, a_i, bur, bui), axis=1)
    y = (jnp.einsum('bsgp,gcp->bsgc', xr, c_re.astype(f32))
         - jnp.einsum('bsgp,gcp->bsgc', xi, c_im.astype(f32))).reshape(B, S, SSM_WIDTH)
    y = jax.nn.gelu(y + d_skip.astype(f32) * uf)
    y = y * jax.nn.sigmoid(y @ glu_w.astype(f32) + glu_b.astype(f32))
    return y.astype(u.dtype)


def moe_dispatch(ht, e_idx, e_w, w1, w3, w2):
    T, D = ht.shape
    n_slots = T * TOP_K_EXPERT
    e_flat = e_idx.reshape(-1)
    w_flat = e_w.reshape(-1)
    tok_flat = jnp.repeat(jnp.arange(T, dtype=jnp.int32), TOP_K_EXPERT)
    order = jnp.argsort(e_flat)
    e_s, tok_s, w_s = e_flat[order], tok_flat[order], w_flat[order]
    counts = jnp.zeros((N_EXPERTS,), jnp.int32).at[e_flat].add(1)
    padded = (counts + MOE_BLOCK - 1) // MOE_BLOCK * MOE_BLOCK
    start = jnp.cumsum(counts) - counts
    pend = jnp.cumsum(padded)
    pstart = pend - padded
    dest = pstart[e_s] + (jnp.arange(n_slots, dtype=jnp.int32) - start[e_s])
    L = n_slots + N_EXPERTS * MOE_BLOCK
    nblk = L // MOE_BLOCK
    buf_tok = jnp.zeros((L,), jnp.int32).at[dest].set(tok_s)
    buf_w = jnp.zeros((L,), w_s.dtype).at[dest].set(w_s)
    blk_start = jnp.arange(nblk, dtype=jnp.int32) * MOE_BLOCK
    blk_e = jnp.minimum(jnp.sum(pend[None, :] <= blk_start[:, None], axis=1), N_EXPERTS - 1)
    xb = ht[buf_tok].reshape(nblk, MOE_BLOCK, D)

    def run_block(args):
        xblk, e = args
        return (jax.nn.silu(xblk @ w1[e]) * (xblk @ w3[e])) @ w2[e]

    yb = lax.map(run_block, (xb, blk_e)).reshape(L, D)
    return jnp.zeros((T, D), yb.dtype).at[buf_tok].add(yb * buf_w[:, None].astype(yb.dtype))


def hier_moe(h, rg_w, rg_b, re_w, re_b, w1, w3, w2):
    B, S, D = h.shape
    ht = h.reshape(B * S, D)
    T = B * S
    pg = jax.nn.softmax((ht @ rg_w).astype(jnp.float32) + rg_b.astype(jnp.float32), axis=-1)
    pg_top, g_idx = lax.top_k(pg, 1)
    el = ((ht @ re_w).astype(jnp.float32) + re_b.astype(jnp.float32)).reshape(T, MOE_GROUPS, EXPERTS_PER_GROUP)
    el_sel = jnp.take_along_axis(el, g_idx[:, :, None], axis=1)[:, 0]
    pe = jax.nn.softmax(el_sel, axis=-1)
    pe_top, e_local = lax.top_k(pe, TOP_K_EXPERT)
    w = pg_top * pe_top / jnp.sum(pe_top, axis=-1, keepdims=True)
    e_idx = g_idx * EXPERTS_PER_GROUP + e_local
    return moe_dispatch(ht, e_idx, w, w1, w3, w2).reshape(B, S, D)


def setup_inputs(seed: int = 0) -> dict:
    key = jax.random.key(seed)
    ks = jax.random.split(key, 40)
    f32 = jnp.float32
    L, D = DEPTH, D_MODEL

    def nrm(k, shape, scale):
        return jax.random.normal(k, shape, f32) * scale

    x = nrm(ks[0], (BATCH, SEQ, D), 1.0)
    c = nrm(ks[1], (BATCH, D), 1.0)
    start = jax.random.randint(ks[2], (BATCH, 1), 0, 16384, dtype=jnp.int32)
    positions = start + jnp.arange(SEQ, dtype=jnp.int32)[None, :]
    n = jnp.arange(SSM_STATE, dtype=f32)
    return {
        "x": x,
        "c": c,
        "positions": positions,
        "ada_w": nrm(ks[3], (L, D, 6 * D), 0.5 * D ** -0.5),
        "ada_b": nrm(ks[4], (L, 6 * D), 0.01),
        "norm1_g": 1.0 + nrm(ks[5], (L, D), 0.02),
        "w_in": nrm(ks[6], (L, D, IN_WIDTH), D ** -0.5),
        "pool_w": nrm(ks[7], (L, POOL_GROUPS, POOL_GROUP_DIM, POOL_GROUP_DIM), POOL_GROUP_DIM ** -0.5),
        "pool_scale": 1.0 + nrm(ks[8], (L, POOL_WIDTH), 0.02),
        "q_norm_g": 1.0 + nrm(ks[9], (L, ATTN_HEAD_DIM), 0.02),
        "k_norm_g": 1.0 + nrm(ks[10], (L, ATTN_HEAD_DIM), 0.02),
        "lambda_q1": nrm(ks[11], (L, ATTN_HEAD_DIM), 0.1),
        "lambda_k1": nrm(ks[12], (L, ATTN_HEAD_DIM), 0.1),
        "lambda_q2": nrm(ks[13], (L, ATTN_HEAD_DIM), 0.1),
        "lambda_k2": nrm(ks[14], (L, ATTN_HEAD_DIM), 0.1),
        "subln_g": 1.0 + nrm(ks[15], (L, ATTN_V_DIM), 0.02),
        "lam_re": -0.5 + nrm(ks[16], (L, SSM_GROUPS, SSM_STATE), 0.01),
        "lam_im": math.pi * n + nrm(ks[17], (L, SSM_GROUPS, SSM_STATE), 0.01),
        "log_dt": jax.random.uniform(ks[18], (L, SSM_GROUPS), f32, math.log(DT_MIN), math.log(DT_MAX)),
        "b_re": nrm(ks[19], (L, SSM_GROUPS, SSM_STATE, SSM_GROUP_DIM), (2 * SSM_GROUP_DIM) ** -0.5),
        "b_im": nrm(ks[20], (L, SSM_GROUPS, SSM_STATE, SSM_GROUP_DIM), (2 * SSM_GROUP_DIM) ** -0.5),
        "c_re": nrm(ks[21], (L, SSM_GROUPS, SSM_GROUP_DIM, SSM_STATE), (2 * SSM_STATE) ** -0.5),
        "c_im": nrm(ks[22], (L, SSM_GROUPS, SSM_GROUP_DIM, SSM_STATE), (2 * SSM_STATE) ** -0.5),
        "d_skip": nrm(ks[23], (L, SSM_WIDTH), 1.0),
        "glu_w": nrm(ks[24], (L, SSM_WIDTH, SSM_WIDTH), SSM_WIDTH ** -0.5),
        "glu_b": nrm(ks[25], (L, SSM_WIDTH), 0.01),
        "proj_a": nrm(ks[26], (L, POOL_WIDTH, D), POOL_WIDTH ** -0.5),
        "proj_b": nrm(ks[27], (L, ATTN_V_WIDTH, D), ATTN_V_WIDTH ** -0.5),
        "proj_c": nrm(ks[28], (L, SSM_WIDTH, D), SSM_WIDTH ** -0.5),
        "gate_w": nrm(ks[29], (L, D, N_BRANCHES * D), D ** -0.5),
        "gate_b": nrm(ks[30], (L, N_BRANCHES * D), 0.01),
        "w_out": nrm(ks[31], (L, D, D), D ** -0.5),
        "norm2_g": 1.0 + nrm(ks[32], (L, D), 0.02),
        "router_g_w": nrm(ks[33], (L, D, MOE_GROUPS), D ** -0.5),
        "router_g_b": nrm(ks[34], (L, MOE_GROUPS), 0.01),
        "router_e_w": nrm(ks[35], (L, D, N_EXPERTS), D ** -0.5),
        "router_e_b": nrm(ks[36], (L, N_EXPERTS), 0.01),
        "moe_w1": nrm(ks[37], (L, N_EXPERTS, D, EXPERT_FF), D ** -0.5),
        "moe_w3": nrm(ks[38], (L, N_EXPERTS, D, EXPERT_FF), D ** -0.5),
        "moe_w2": nrm(ks[39], (L, N_EXPERTS, EXPERT_FF, D), EXPERT_FF ** -0.5),
    }


def reference(x, c, positions, ada_w, ada_b, norm1_g, w_in, pool_w, pool_scale,
              q_norm_g, k_norm_g, lambda_q1, lambda_k1, lambda_q2, lambda_k2, subln_g,
              lam_re, lam_im, log_dt, b_re, b_im, c_re, c_im, d_skip, glu_w, glu_b,
              proj_a, proj_b, proj_c, gate_w, gate_b, w_out, norm2_g,
              router_g_w, router_g_b, router_e_w, router_e_b, moe_w1, moe_w3, moe_w2):
    cos, sin = rope_tables(positions)
    c_act = jax.nn.silu(c)
    splits = [POOL_WIDTH, POOL_WIDTH + ATTN_QK_WIDTH, POOL_WIDTH + 2 * ATTN_QK_WIDTH,
              POOL_WIDTH + 2 * ATTN_QK_WIDTH + ATTN_V_WIDTH]
    for l in range(DEPTH):
        lambda_init = 0.8 - 0.6 * math.exp(-0.3 * l)
        mod = c_act @ ada_w[l] + ada_b[l]
        sh1, sc1, g1, sh2, sc2, g2 = jnp.split(mod, 6, axis=-1)
        h = modulate(rmsnorm(x, norm1_g[l]), sh1, sc1)
        z = h @ w_in[l]
        u_a, q, k, v, u_c = jnp.split(z, splits, axis=-1)
        ya = pool_mixer(u_a, pool_w[l], pool_scale[l])
        yb = diff_attention(q, k, v, cos, sin, q_norm_g[l], k_norm_g[l], lambda_q1[l], lambda_k1[l],
                            lambda_q2[l], lambda_k2[l], subln_g[l], lambda_init)
        yc = ssm_mixer(u_c, lam_re[l], lam_im[l], log_dt[l], b_re[l], b_im[l], c_re[l], c_im[l],
                       d_skip[l], glu_w[l], glu_b[l])
        ga, gb, gc = jnp.split(jax.nn.sigmoid(h @ gate_w[l] + gate_b[l]), N_BRANCHES, axis=-1)
        mixed = ga * (ya @ proj_a[l]) + gb * (yb @ proj_b[l]) + gc * (yc @ proj_c[l])
        x = x + g1[:, None, :] * (mixed @ w_out[l])
        h2 = modulate(rmsnorm(x, norm2_g[l]), sh2, sc2)
        x = x + g2[:, None, :] * hier_moe(h2, router_g_w[l], router_g_b[l], router_e_w[l], router_e_b[l],
                                          moe_w1[l], moe_w3[l], moe_w2[l])
    return x
```

```python
import functools
import math

import jax
import jax.numpy as jnp
from jax import lax
from jax.experimental import pallas as pl
from jax.experimental.pallas import tpu as pltpu

F32 = jnp.float32
BF16 = jnp.bfloat16

NORM_EPS = 1e-6
CHUNK = 64
POOL_WINDOWS = (2, 4, 8, 16)
POOL_HIST = 16
HEAD_DIM = 64
N_HEADS = 4
HEAD_LANES = 2 * HEAD_DIM
ROPE_THETA = 10000.0
SSM_GROUP_DIM = 16
SSM_STATE = 64
MOE_GROUPS = 4
EXPERTS_PER_GROUP = 8
N_EXPERTS = MOE_GROUPS * EXPERTS_PER_GROUP
ROUTE_LANES = 128
NEG_BIG = -1e30
LANES = 128

VMEM_LIMIT = 48 * 1024 * 1024

SEQ_TILE = 512
ATTN_TILE = 512
SSM_TILE = 256
MOE_TILE = 512
COMB_TILE = 256


def _cparams(sem):
    return pltpu.CompilerParams(dimension_semantics=sem, vmem_limit_bytes=VMEM_LIMIT)


def _rms(x, g):
    return x * lax.rsqrt(jnp.mean(x * x, axis=-1, keepdims=True) + NORM_EPS) * g


def _dot(a, b):
    return jnp.dot(a, b, preferred_element_type=F32)


def _ada_kernel(c_ref, w_ref, b_ref, o_ref):
    c = c_ref[...]
    act = (c * jax.nn.sigmoid(c)).astype(BF16)
    o_ref[0] = _dot(act, w_ref[0].astype(BF16)) + b_ref[0]


def _ada_mod(c, ada_w, ada_b):
    depth, d, n = ada_w.shape
    nb = c.shape[0]
    tn = 1536
    return pl.pallas_call(
        _ada_kernel,
        grid=(depth, n // tn),
        in_specs=[
            pl.BlockSpec((nb, d), lambda l, j: (0, 0)),
            pl.BlockSpec((1, d, tn), lambda l, j: (l, 0, j)),
            pl.BlockSpec((1, 1, tn), lambda l, j: (l, 0, j)),
        ],
        out_specs=pl.BlockSpec((1, nb, tn), lambda l, j: (l, 0, j)),
        out_shape=jax.ShapeDtypeStruct((depth, nb, n), F32),
        compiler_params=_cparams(("arbitrary", "arbitrary")),
        name="ada_mod",
    )(c, ada_w, ada_b.reshape(depth, 1, n))


def _qk_prep(t, gain, ones, cosf, sinf, scale):
    t2 = t * t
    hi = t2.astype(BF16)
    lo = (t2 - hi.astype(F32)).astype(BF16)
    lane = lax.broadcasted_iota(jnp.int32, (t.shape[0], HEAD_LANES), 1)
    first_half = (lane % HEAD_DIM) < (HEAD_DIM // 2)
    outs = []
    for half in range(2):
        sl = slice(half * 256, half * 256 + 256)
        ss = _dot(hi[:, sl], ones) + _dot(lo[:, sl], ones)
        tn = t[:, sl] * lax.rsqrt(ss * (1.0 / HEAD_DIM) + NORM_EPS)
        for j in range(2):
            th = tn[:, j * HEAD_LANES:(j + 1) * HEAD_LANES] * gain
            rot = jnp.where(first_half, pltpu.roll(th, HEAD_LANES - HEAD_DIM // 2, 1),
                            pltpu.roll(th, HEAD_DIM // 2, 1))
            outs.append(((th * cosf + rot * sinf) * scale).astype(BF16))
    return jnp.concatenate(outs, axis=1)


def _inproj_kernel(x_ref, mod_ref, g_ref, w_ref, qg_ref, kg_ref, ones_ref, cos_ref, sin_ref,
                   ua_ref, q_ref, k_ref, v_ref, uc_ref):
    x = x_ref[0]
    m = mod_ref[0]
    h = _rms(x, g_ref[...]) * (1.0 + m[1:2]) + m[0:1]
    z = _dot(h.astype(BF16), w_ref[...])
    ua_ref[0] = z[:, 0:256]
    uc_ref[...] = z[:, 1792:2048]
    v_ref[0] = z[:, 1280:1792].astype(BF16)
    cosf = cos_ref[0]
    sinf = sin_ref[0]
    ones = ones_ref[...]
    q_ref[0] = _qk_prep(z[:, 256:768], qg_ref[...], ones, cosf, sinf, HEAD_DIM ** -0.5)
    k_ref[0] = _qk_prep(z[:, 768:1280], kg_ref[...], ones, cosf, sinf, 1.0)


def _inproj(x, mod, g, w_in, qg, kg, ones, cosf, sinf):
    nb, s, d = x.shape
    ts = min(SEQ_TILE, s)
    full = lambda shape: pl.BlockSpec(shape, lambda b, t: (0,) * len(shape))
    return pl.pallas_call(
        _inproj_kernel,
        grid=(nb, s // ts),
        in_specs=[
            pl.BlockSpec((1, ts, d), lambda b, t: (b, t, 0)),
            pl.BlockSpec((1, 6, d), lambda b, t: (b, 0, 0)),
            full((1, d)),
            full(w_in.shape),
            full((1, HEAD_LANES)),
            full((1, HEAD_LANES)),
            full((256, 256)),
            pl.BlockSpec((1, ts, HEAD_LANES), lambda b, t: (b, t, 0)),
            pl.BlockSpec((1, ts, HEAD_LANES), lambda b, t: (b, t, 0)),
        ],
        out_specs=[
            pl.BlockSpec((1, ts, 256), lambda b, t: (b, t, 0)),
            pl.BlockSpec((1, ts, 512), lambda b, t: (b, t, 0)),
            pl.BlockSpec((1, ts, 512), lambda b, t: (b, t, 0)),
            pl.BlockSpec((1, ts, 512), lambda b, t: (b, t, 0)),
            pl.BlockSpec((ts, 256), lambda b, t: (t, b)),
        ],
        out_shape=[
            jax.ShapeDtypeStruct((nb, s, 256), F32),
            jax.ShapeDtypeStruct((nb, s, 512), BF16),
            jax.ShapeDtypeStruct((nb, s, 512), BF16),
            jax.ShapeDtypeStruct((nb, s, 512), BF16),
            jax.ShapeDtypeStruct((s, nb * 256), F32),
        ],
        compiler_params=_cparams(("parallel", "parallel")),
        name="inproj",
    )(x, mod, g, w_in, qg, kg, ones, cosf, sinf)


def _attn_kernel(qi_tab, ki_tab, lam_ref, q_ref, k_ref, v_ref, sg_ref, o_ref,
                 q2_ref, m_ref, l_ref, acc_ref, *, tq, out_scale):
    p = pl.program_id(2)
    qi = qi_tab[p]
    ki = ki_tab[p]

    @pl.when(ki == 0)
    def _():
        q = q_ref[0]
        lane = lax.broadcasted_iota(jnp.int32, q.shape, 1)
        zero = jnp.zeros_like(q)
        q2_ref[0:tq, :] = jnp.where(lane < HEAD_DIM, q, zero)
        q2_ref[tq:2 * tq, :] = jnp.where(lane >= HEAD_DIM, q, zero)
        m_ref[...] = jnp.full(m_ref.shape, NEG_BIG, F32)
        l_ref[...] = jnp.zeros(l_ref.shape, F32)
        acc_ref[...] = jnp.zeros(acc_ref.shape, F32)

    def step(masked):
        s = lax.dot_general(q2_ref[...], k_ref[0], (((1,), (1,)), ((), ())),
                            preferred_element_type=F32)
        if masked:
            row = lax.broadcasted_iota(jnp.int32, s.shape, 0) % tq
            col = lax.broadcasted_iota(jnp.int32, s.shape, 1)
            s = jnp.where((col // CHUNK) <= (row // CHUNK), s, NEG_BIG)
        m_old = m_ref[...]
        m_new = jnp.maximum(m_old, jnp.max(s, axis=1, keepdims=True))
        alpha = jnp.exp(m_old - m_new)
        pr = jnp.exp(s - m_new)
        l_ref[...] = alpha * l_ref[...] + jnp.sum(pr, axis=1, keepdims=True)
        acc_ref[...] = alpha * acc_ref[...] + _dot(pr.astype(BF16), v_ref[0])
        m_ref[...] = m_new

    @pl.when(ki < qi)
    def _():
        step(False)

    @pl.when(ki == qi)
    def _():
        step(True)
        acc = acc_ref[...]
        l = l_ref[...]
        o = acc[0:tq] / l[0:tq] - lam_ref[0] * (acc[tq:2 * tq] / l[tq:2 * tq])
        o_ref[0] = (_rms(o, sg_ref[...]) * out_scale).astype(BF16)


def _attention(q, k, v, lam, subln_g, lambda_init):
    nb, s, _ = q.shape
    tq = min(ATTN_TILE, s)
    nq = s // tq
    pairs = [(i, j) for i in range(nq) for j in range(i + 1)]
    qi_tab = jnp.asarray([a for a, _ in pairs], jnp.int32)
    ki_tab = jnp.asarray([b for _, b in pairs], jnp.int32)
    grid_spec = pltpu.PrefetchScalarGridSpec(
        num_scalar_prefetch=2,
        grid=(nb, N_HEADS, len(pairs)),
        in_specs=[
            pl.BlockSpec(memory_space=pltpu.SMEM),
            pl.BlockSpec((1, tq, HEAD_LANES), lambda b, h, p, qt, kt: (b, qt[p], h)),
            pl.BlockSpec((1, tq, HEAD_LANES), lambda b, h, p, qt, kt: (b, kt[p], h)),
            pl.BlockSpec((1, tq, HEAD_LANES), lambda b, h, p, qt, kt: (b, kt[p], h)),
            pl.BlockSpec((1, HEAD_LANES), lambda b, h, p, qt, kt: (0, 0)),
        ],
        out_specs=pl.BlockSpec((1, tq, HEAD_LANES), lambda b, h, p, qt, kt: (b, qt[p], h)),
        scratch_shapes=[
            pltpu.VMEM((2 * tq, HEAD_LANES), BF16),
            pltpu.VMEM((2 * tq, 1), F32),
            pltpu.VMEM((2 * tq, 1), F32),
            pltpu.VMEM((2 * tq, HEAD_LANES), F32),
        ],
    )
    return pl.pallas_call(
        functools.partial(_attn_kernel, tq=tq, out_scale=1.0 - lambda_init),
        grid_spec=grid_spec,
        out_shape=jax.ShapeDtypeStruct((nb, s, N_HEADS * HEAD_LANES), BF16),
        compiler_params=_cparams(("parallel", "parallel", "arbitrary")),
        name="diff_attn",
    )(qi_tab, ki_tab, lam, q, k, v, subln_g)


def _gelu_tanh(x):
    return 0.5 * x * (1.0 + jnp.tanh(math.sqrt(2.0 / math.pi) * (x + 0.044715 * (x * x * x))))


def _ssm_kernel(u_ref, bb_ref, a_ref, ct_ref, dskip_ref, gw_ref, gb_ref, y_ref,
                utm_ref, bur_ref, bui_ref, ytm_ref, xr_ref, xi_ref, *, tt, nb, width, nstate):
    @pl.when(pl.program_id(0) == 0)
    def _():
        xr_ref[...] = jnp.zeros(xr_ref.shape, F32)
        xi_ref[...] = jnp.zeros(xi_ref.shape, F32)

    nlt = width // LANES
    for b in range(nb):
        for j in range(nlt):
            utm_ref[j, pl.ds(b, tt, stride=nb), :] = u_ref[:, b * width + j * LANES:b * width + (j + 1) * LANES]
    u_tm = jnp.concatenate([utm_ref[j] for j in range(nlt)], axis=1)
    bu = _dot(u_tm.astype(BF16), bb_ref[...])
    bur_ref[...] = bu[:, :nstate]
    bui_ref[...] = bu[:, nstate:]

    ar = jnp.broadcast_to(a_ref[0:1, :], (nb, nstate))
    ai = jnp.broadcast_to(a_ref[1:2, :], (nb, nstate))

    def body(t, carry):
        xr, xi = carry
        r0 = pl.multiple_of(t * nb, nb)
        nxr = ar * xr - ai * xi + bur_ref[pl.ds(r0, nb), :]
        nxi = ar * xi + ai * xr + bui_ref[pl.ds(r0, nb), :]
        bur_ref[pl.ds(r0, nb), :] = nxr
        bui_ref[pl.ds(r0, nb), :] = nxi
        return nxr, nxi

    xr, xi = lax.fori_loop(0, tt, body, (xr_ref[...], xi_ref[...]), unroll=8)
    xr_ref[...] = xr
    xi_ref[...] = xi

    y = (_dot(bur_ref[...].astype(BF16), ct_ref[0:nstate, :])
         + _dot(bui_ref[...].astype(BF16), ct_ref[nstate:2 * nstate, :]))
    y = _gelu_tanh(y + dskip_ref[...] * u_tm)
    y = y * jax.nn.sigmoid(_dot(y.astype(BF16), gw_ref[...]) + gb_ref[...])
    for j in range(nlt):
        ytm_ref[j] = y[:, j * LANES:(j + 1) * LANES]
    for b in range(nb):
        for j in range(nlt):
            y_ref[:, b * width + j * LANES:b * width + (j + 1) * LANES] = ytm_ref[j, pl.ds(b, tt, stride=nb), :]


def _ssm(u_tm, bb, a, ct, dskip, gw, gb, nb):
    s = u_tm.shape[0]
    width = u_tm.shape[1] // nb
    nstate = a.shape[1]
    tt = min(SSM_TILE, s)
    full = lambda arr: pl.BlockSpec(arr.shape, lambda c: (0,) * arr.ndim)
    return pl.pallas_call(
        functools.partial(_ssm_kernel, tt=tt, nb=nb, width=width, nstate=nstate),
        grid=(s // tt,),
        in_specs=[pl.BlockSpec((tt, nb * width), lambda c: (c, 0)),
                  full(bb), full(a), full(ct), full(dskip), full(gw), full(gb)],
        out_specs=pl.BlockSpec((tt, nb * width), lambda c: (c, 0)),
        out_shape=jax.ShapeDtypeStruct(u_tm.shape, F32),
        scratch_shapes=[
            pltpu.VMEM((width // LANES, tt * nb, LANES), F32),
            pltpu.VMEM((tt * nb, nstate), F32),
            pltpu.VMEM((tt * nb, nstate), F32),
            pltpu.VMEM((width // LANES, tt * nb, LANES), F32),
            pltpu.VMEM((nb, nstate), F32),
            pltpu.VMEM((nb, nstate), F32),
        ],
        compiler_params=_cparams(("arbitrary",)),
        name="ssm",
    )(u_tm, bb, a, ct, dskip, gw, gb)


def _route(logits):
    lane = lax.broadcasted_iota(jnp.int32, logits.shape, 1)
    far = jnp.int32(4 * ROUTE_LANES)
    gl = jnp.where(lane < MOE_GROUPS, logits, NEG_BIG)
    gmax = jnp.max(gl, axis=1, keepdims=True)
    gidx = jnp.min(jnp.where(gl == gmax, lane, far), axis=1, keepdims=True)
    pg_top = 1.0 / jnp.sum(jnp.exp(gl - gmax), axis=1, keepdims=True)
    in_group = (lane >= MOE_GROUPS) & (((lane - MOE_GROUPS) // EXPERTS_PER_GROUP) == gidx) \
        & (lane < MOE_GROUPS + N_EXPERTS)
    el = jnp.where(in_group, logits, NEG_BIG)
    m1 = jnp.max(el, axis=1, keepdims=True)
    i1 = jnp.min(jnp.where(el == m1, lane, far), axis=1, keepdims=True)
    el2 = jnp.where(lane == i1, NEG_BIG, el)
    m2 = jnp.max(el2, axis=1, keepdims=True)
    i2 = jnp.min(jnp.where(el2 == m2, lane, far), axis=1, keepdims=True)
    e2 = jnp.exp(m2 - m1)
    w1 = pg_top / (1.0 + e2)
    w2 = pg_top * e2 / (1.0 + e2)
    idx = jnp.where(lane == 0, i1 - MOE_GROUPS, jnp.where(lane == 1, i2 - MOE_GROUPS, 0))
    wts = jnp.where(lane == 0, w1, jnp.where(lane == 1, w2, 0.0))
    return idx, wts


def _merge_kernel(x_ref, mod_ref, ua_ref, o_ref, yc_ref, g1_ref, gw_ref, gb_ref, pw_ref, ps_ref,
                  pa_ref, pb_ref, pc_ref, wo_ref, g2_ref, rw_ref, rb_ref,
                  x1_ref, h2_ref, ri_ref, rwt_ref, ubuf_ref, *, ts, d):
    ti = pl.program_id(1)
    x = x_ref[0]
    m = mod_ref[0]
    h = _rms(x, g1_ref[...]) * (1.0 + m[1:2]) + m[0:1]
    gates = jax.nn.sigmoid(_dot(h.astype(BF16), gw_ref[...]) + gb_ref[...])

    @pl.when(ti == 0)
    def _():
        ubuf_ref[0:POOL_HIST, :] = jnp.zeros((POOL_HIST, ubuf_ref.shape[1]), F32)

    u = ua_ref[0]
    ubuf_ref[POOL_HIST:POOL_HIST + ts, :] = u
    t_abs = (ti * ts + lax.broadcasted_iota(jnp.int32, (ts, 1), 0) + 1).astype(F32)
    lane = lax.broadcasted_iota(jnp.int32, u.shape, 1)
    grp = lane // (u.shape[1] // len(POOL_WINDOWS))
    run = jnp.zeros_like(u)
    pooled = jnp.zeros_like(u)
    for j in range(POOL_HIST):
        run = run + ubuf_ref[pl.ds(POOL_HIST - j, ts), :]
        if (j + 1) in POOL_WINDOWS:
            g = POOL_WINDOWS.index(j + 1)
            val = run / jnp.minimum(t_abs, float(j + 1)) - u
            pooled = jnp.where(grp == g, val, pooled)
    ubuf_ref[0:POOL_HIST, :] = ubuf_ref[ts:ts + POOL_HIST, :]
    ya = _dot(pooled.astype(BF16), pw_ref[...]) * ps_ref[...]

    mixed = (gates[:, 0:d] * _dot(ya.astype(BF16), pa_ref[...])
             + gates[:, d:2 * d] * _dot(o_ref[0], pb_ref[...])
             + gates[:, 2 * d:3 * d] * _dot(yc_ref[...].astype(BF16), pc_ref[...]))
    x1 = x + m[2:3] * _dot(mixed.astype(BF16), wo_ref[...])
    x1_ref[0] = x1
    h2 = _rms(x1, g2_ref[...]) * (1.0 + m[4:5]) + m[3:4]
    h2_ref[0] = h2
    logits = _dot(h2.astype(BF16), rw_ref[...]) + rb_ref[...]
    idx, wts = _route(logits)
    ri_ref[0] = idx
    rwt_ref[0] = wts


def _merge(x, mod, ua, o, yc_tm, g1, gw, gb, pw, ps, pa, pb, pc, wo, g2, rw, rb):
    nb, s, d = x.shape
    ts = min(SEQ_TILE, s)
    full = lambda arr: pl.BlockSpec(arr.shape, lambda b, t: (0,) * arr.ndim)
    tile = lambda w: pl.BlockSpec((1, ts, w), lambda b, t: (b, t, 0))
    return pl.pallas_call(
        functools.partial(_merge_kernel, ts=ts, d=d),
        grid=(nb, s // ts),
        in_specs=[tile(d), pl.BlockSpec((1, 6, d), lambda b, t: (b, 0, 0)), tile(256), tile(512),
                  pl.BlockSpec((ts, 256), lambda b, t: (t, b)),
                  full(g1), full(gw), full(gb), full(pw), full(ps), full(pa), full(pb), full(pc),
                  full(wo), full(g2), full(rw), full(rb)],
        out_specs=[tile(d), tile(d), tile(ROUTE_LANES), tile(ROUTE_LANES)],
        out_shape=[
            jax.ShapeDtypeStruct((nb, s, d), F32),
            jax.ShapeDtypeStruct((nb, s, d), F32),
            jax.ShapeDtypeStruct((nb, s, ROUTE_LANES), jnp.int32),
            jax.ShapeDtypeStruct((nb, s, ROUTE_LANES), F32),
        ],
        scratch_shapes=[pltpu.VMEM((POOL_HIST + ts, 256), F32)],
        compiler_params=_cparams(("arbitrary", "arbitrary")),
        name="merge_route",
    )(x, mod, ua, o, yc_tm, g1, gw, gb, pw, ps, pa, pb, pc, wo, g2, rw, rb)


def _row_copy(src_hbm, row, dst, r, sem):
    return pltpu.make_async_copy(src_hbm.at[pl.ds(row, 1), :], dst.at[pl.ds(r, 1), :], sem)


def _expert_kernel(blk_e, first, nused, tok_ref, h_hbm, w1_ref, w3_ref, w2_ref, y_ref,
                   xbuf, w1b, w3b, w2b, sem, *, tmb):
    i = pl.program_id(0)

    @pl.when(i < nused[0])
    def _():
        def issue(r, c):
            _row_copy(h_hbm, tok_ref[r], xbuf, r, sem).start()
            return c

        lax.fori_loop(0, tmb, issue, 0)

        @pl.when(first[i] == 1)
        def _():
            w1b[...] = w1_ref[0].astype(BF16)
            w3b[...] = w3_ref[0].astype(BF16)
            w2b[...] = w2_ref[0].astype(BF16)

        def drain(r, c):
            _row_copy(h_hbm, 0, xbuf, r, sem).wait()
            return c

        lax.fori_loop(0, tmb, drain, 0)
        xb = xbuf[...].astype(BF16)
        a = _dot(xb, w1b[...])
        g = _dot(xb, w3b[...])
        mid = (a * jax.nn.sigmoid(a)) * g
        y_ref[...] = _dot(mid.astype(BF16), w2b[...])

    @pl.when(i >= nused[0])
    def _():
        y_ref[...] = jnp.zeros(y_ref.shape, F32)


def _experts(h2, buf_tok, blk_e, first, nused, w1, w3, w2):
    t, d = h2.shape
    nslot = buf_tok.shape[0]
    tmb = MOE_TILE
    ff = w1.shape[2]
    grid_spec = pltpu.PrefetchScalarGridSpec(
        num_scalar_prefetch=3,
        grid=(nslot // tmb,),
        in_specs=[
            pl.BlockSpec((tmb,), lambda i, be, fi, nu: (i,), memory_space=pltpu.SMEM),
            pl.BlockSpec(memory_space=pl.ANY),
            pl.BlockSpec((1, d, ff), lambda i, be, fi, nu: (be[i], 0, 0)),
            pl.BlockSpec((1, d, ff), lambda i, be, fi, nu: (be[i], 0, 0)),
            pl.BlockSpec((1, ff, d), lambda i, be, fi, nu: (be[i], 0, 0)),
        ],
        out_specs=pl.BlockSpec((tmb, d), lambda i, be, fi, nu: (i, 0)),
        scratch_shapes=[
            pltpu.VMEM((tmb, d), F32),
            pltpu.VMEM((d, ff), BF16),
            pltpu.VMEM((d, ff), BF16),
            pltpu.VMEM((ff, d), BF16),
            pltpu.SemaphoreType.DMA,
        ],
    )
    return pl.pallas_call(
        functools.partial(_expert_kernel, tmb=tmb),
        grid_spec=grid_spec,
        out_shape=jax.ShapeDtypeStruct((nslot, d), F32),
        compiler_params=_cparams(("arbitrary",)),
        name="experts",
    )(blk_e, first, nused, buf_tok, h2, w1, w3, w2)


def _combine_kernel(d0_ref, d1_ref, x1_ref, mod_ref, w_ref, y_hbm, o_ref, r0buf, r1buf, sem, *, tcb):
    def issue(r, c):
        _row_copy(y_hbm, d0_ref[r], r0buf, r, sem).start()
        _row_copy(y_hbm, d1_ref[r], r1buf, r, sem).start()
        return c

    lax.fori_loop(0, tcb, issue, 0)

    def drain(r, c):
        _row_copy(y_hbm, 0, r0buf, r, sem).wait()
        _row_copy(y_hbm, 0, r1buf, r, sem).wait()
        return c

    lax.fori_loop(0, tcb, drain, 0)
    w = w_ref[...]
    moe = w[:, 0:1] * r0buf[...] + w[:, 1:2] * r1buf[...]
    o_ref[...] = x1_ref[...] + mod_ref[0][5:6] * moe


def _combine(x1, mod, wts, dest0, dest1, yb, seq):
    t, d = x1.shape
    tcb = min(COMB_TILE, seq)
    per_b = seq // tcb
    return pl.pallas_call(
        functools.partial(_combine_kernel, tcb=tcb),
        grid=(t // tcb,),
        in_specs=[
            pl.BlockSpec((tcb,), lambda i: (i,), memory_space=pltpu.SMEM),
            pl.BlockSpec((tcb,), lambda i: (i,), memory_space=pltpu.SMEM),
            pl.BlockSpec((tcb, d), lambda i: (i, 0)),
            pl.BlockSpec((1, 6, d), lambda i: (i // per_b, 0, 0)),
            pl.BlockSpec((tcb, ROUTE_LANES), lambda i: (i, 0)),
            pl.BlockSpec(memory_space=pl.ANY),
        ],
        out_specs=pl.BlockSpec((tcb, d), lambda i: (i, 0)),
        out_shape=jax.ShapeDtypeStruct((t, d), F32),
        scratch_shapes=[pltpu.VMEM((tcb, d), F32), pltpu.VMEM((tcb, d), F32), pltpu.SemaphoreType.DMA],
        compiler_params=_cparams(("arbitrary",)),
        name="combine",
    )(dest0, dest1, x1, mod, wts, yb)


def _dispatch_plan(e_idx, tmb):
    t = e_idx.shape[0]
    e_flat = e_idx.reshape(-1)
    onehot = (e_flat[:, None] == jnp.arange(N_EXPERTS, dtype=jnp.int32)[None, :]).astype(jnp.int32)
    csum = jnp.cumsum(onehot, axis=0)
    rank = jnp.take_along_axis(csum, e_flat[:, None], axis=1)[:, 0] - 1
    counts = csum[-1]
    padded = (counts + tmb - 1) // tmb * tmb
    pend = jnp.cumsum(padded)
    pstart = pend - padded
    dest = pstart[e_flat] + rank
    nslot = 2 * t + N_EXPERTS * tmb
    tok_flat = jnp.repeat(jnp.arange(t, dtype=jnp.int32), 2)
    buf_tok = jnp.zeros((nslot,), jnp.int32).at[dest].set(tok_flat)
    blk_start = jnp.arange(nslot // tmb, dtype=jnp.int32) * tmb
    blk_e = jnp.minimum(jnp.sum(pend[None, :] <= blk_start[:, None], axis=1), N_EXPERTS - 1).astype(jnp.int32)
    first = jnp.concatenate([jnp.ones((1,), jnp.int32), (blk_e[1:] != blk_e[:-1]).astype(jnp.int32)])
    nused = (pend[-1] // tmb).astype(jnp.int32).reshape(1)
    dest2 = dest.reshape(t, 2)
    return buf_tok, blk_e, first, nused, dest2[:, 0], dest2[:, 1]


def _block_diag(blocks):
    g, r, c = blocks.shape
    eye = jnp.eye(g, dtype=blocks.dtype)
    return jnp.einsum('grc,gh->grhc', blocks, eye).reshape(g * r, g * c)


def _ssm_params(lam_re, lam_im, log_dt, b_re, b_im, c_re, c_im):
    dt = jnp.exp(log_dt)[:, None]
    mag = jnp.exp(lam_re * dt)
    ar, ai = mag * jnp.cos(lam_im * dt), mag * jnp.sin(lam_im * dt)
    nr, ni = ar - 1.0, ai
    den = lam_re * lam_re + lam_im * lam_im
    kr, ki = (nr * lam_re + ni * lam_im) / den, (ni * lam_re - nr * lam_im) / den
    bbr = kr[..., None] * b_re - ki[..., None] * b_im
    bbi = kr[..., None] * b_im + ki[..., None] * b_re
    bb = jnp.concatenate([_block_diag(jnp.swapaxes(bbr, 1, 2)), _block_diag(jnp.swapaxes(bbi, 1, 2))], axis=1)
    ct = jnp.concatenate([_block_diag(jnp.swapaxes(c_re, 1, 2)), -_block_diag(jnp.swapaxes(c_im, 1, 2))], axis=0)
    a = jnp.stack([ar.reshape(-1), ai.reshape(-1)], axis=0)
    return bb.astype(BF16), a, ct.astype(BF16)


def _rope_tables(positions):
    inv = ROPE_THETA ** (-jnp.arange(0, HEAD_DIM, 2, dtype=F32) / HEAD_DIM)
    ang = positions.astype(F32)[..., None] * inv
    cos, sin = jnp.cos(ang), jnp.sin(ang)
    cosf = jnp.concatenate([cos, cos, cos, cos], axis=-1)
    sinf = jnp.concatenate([-sin, sin, -sin, sin], axis=-1)
    return cosf, sinf


def kernel(x, c, positions, ada_w, ada_b, norm1_g, w_in, pool_w, pool_scale, q_norm_g, k_norm_g, lambda_q1, lambda_k1, lambda_q2, lambda_k2, subln_g, lam_re, lam_im, log_dt, b_re, b_im, c_re, c_im, d_skip, glu_w, glu_b, proj_a, proj_b, proj_c, gate_w, gate_b, w_out, norm2_g, router_g_w, router_g_b, router_e_w, router_e_b, moe_w1, moe_w3, moe_w2):
    nb, s, d = x.shape
    depth = ada_w.shape[0]
    cosf, sinf = _rope_tables(positions)
    mod_all = _ada_mod(c, ada_w, ada_b).reshape(depth, nb, 6, d)
    ones = _block_diag(jnp.ones((256 // HEAD_DIM, HEAD_DIM, HEAD_DIM), F32)).astype(BF16)
    row = lambda v: v.reshape(1, -1)
    tile2 = lambda v: jnp.concatenate([v, v]).reshape(1, -1)
    for l in range(depth):
        lambda_init = 0.8 - 0.6 * math.exp(-0.3 * l)
        mod = mod_all[l]
        ua, q, k, v, uc_tm = _inproj(x, mod, row(norm1_g[l]), w_in[l].astype(BF16),
                                     tile2(q_norm_g[l]), tile2(k_norm_g[l]), ones, cosf, sinf)
        lam = (jnp.exp(jnp.sum(lambda_q1[l] * lambda_k1[l])) - jnp.exp(jnp.sum(lambda_q2[l] * lambda_k2[l]))
               + lambda_init).reshape(1).astype(F32)
        o = _attention(q, k, v, lam, row(subln_g[l]), lambda_init)
        bb, a, ct = _ssm_params(lam_re[l], lam_im[l], log_dt[l], b_re[l], b_im[l], c_re[l], c_im[l])
        yc_tm = _ssm(uc_tm, bb, a, ct, row(d_skip[l]), glu_w[l].astype(BF16), row(glu_b[l]), nb)
        rw = jnp.zeros((d, ROUTE_LANES), F32).at[:, 0:MOE_GROUPS].set(router_g_w[l]) \
            .at[:, MOE_GROUPS:MOE_GROUPS + N_EXPERTS].set(router_e_w[l]).astype(BF16)
        rb = jnp.zeros((1, ROUTE_LANES), F32).at[0, 0:MOE_GROUPS].set(router_g_b[l]) \
            .at[0, MOE_GROUPS:MOE_GROUPS + N_EXPERTS].set(router_e_b[l])
        x1, h2, ri, rwt = _merge(x, mod, ua, o, yc_tm, row(norm1_g[l]), gate_w[l].astype(BF16), row(gate_b[l]),
                                 _block_diag(pool_w[l]).astype(BF16), row(pool_scale[l]),
                                 proj_a[l].astype(BF16), proj_b[l].astype(BF16), proj_c[l].astype(BF16),
                                 w_out[l].astype(BF16), row(norm2_g[l]), rw, rb)
        t = nb * s
        buf_tok, blk_e, first, nused, dest0, dest1 = _dispatch_plan(ri.reshape(t, ROUTE_LANES)[:, 0:2], MOE_TILE)
        yb = _experts(h2.reshape(t, d), buf_tok, blk_e, first, nused, moe_w1[l], moe_w3[l], moe_w2[l])
        x = _combine(x1.reshape(t, d), mod, rwt.reshape(t, ROUTE_LANES), dest0, dest1, yb, s).reshape(nb, s, d)
    return x
```

```python
import functools
import math

import jax
import jax.numpy as jnp
from jax import lax
from jax.experimental import pallas as pl
from jax.experimental.pallas import tpu as pltpu

F32 = jnp.float32
BF16 = jnp.bfloat16

NORM_EPS = 1e-6
CHUNK = 64
POOL_WINDOWS = (2, 4, 8, 16)
POOL_HIST = 16
HEAD_DIM = 64
N_HEADS = 4
HEAD_LANES = 2 * HEAD_DIM
ROPE_THETA = 10000.0
SSM_GROUP_DIM = 16
SSM_STATE = 64
MOE_GROUPS = 4
EXPERTS_PER_GROUP = 8
N_EXPERTS = MOE_GROUPS * EXPERTS_PER_GROUP
ROUTE_LANES = 128
NEG_BIG = -1e30
LANES = 128
LOG2E = math.log2(math.e)
SHIFT_FREE_LIMIT = 60.0

VMEM_LIMIT = 48 * 1024 * 1024

SEQ_TILE = 512
ATTN_TILE = 512
SSM_TILE = 256
MOE_TILE = 512
COMB_TILE = 256


def _cparams(sem):
    return pltpu.CompilerParams(dimension_semantics=sem, vmem_limit_bytes=VMEM_LIMIT)


def _rms(x, g):
    return x * lax.rsqrt(jnp.mean(x * x, axis=-1, keepdims=True) + NORM_EPS) * g


def _dot(a, b):
    return jnp.dot(a, b, preferred_element_type=F32)


def _ada_kernel(c_ref, w_ref, b_ref, o_ref):
    c = c_ref[...]
    act = (c * jax.nn.sigmoid(c)).astype(BF16)
    o_ref[0] = _dot(act, w_ref[0].astype(BF16)) + b_ref[0]


def _ada_mod(c, ada_w, ada_b):
    depth, d, n = ada_w.shape
    nb = c.shape[0]
    tn = 1536
    return pl.pallas_call(
        _ada_kernel,
        grid=(depth, n // tn),
        in_specs=[
            pl.BlockSpec((nb, d), lambda l, j: (0, 0)),
            pl.BlockSpec((1, d, tn), lambda l, j: (l, 0, j)),
            pl.BlockSpec((1, 1, tn), lambda l, j: (l, 0, j)),
        ],
        out_specs=pl.BlockSpec((1, nb, tn), lambda l, j: (l, 0, j)),
        out_shape=jax.ShapeDtypeStruct((depth, nb, n), F32),
        compiler_params=_cparams(("arbitrary", "arbitrary")),
        name="ada_mod",
    )(c, ada_w, ada_b.reshape(depth, 1, n))


def _qk_prep(t, gain, ones, cosf, sinf, scale):
    t2 = t * t
    hi = t2.astype(BF16)
    lo = (t2 - hi.astype(F32)).astype(BF16)
    lane = lax.broadcasted_iota(jnp.int32, (t.shape[0], HEAD_LANES), 1)
    first_half = (lane % HEAD_DIM) < (HEAD_DIM // 2)
    outs = []
    for half in range(2):
        sl = slice(half * 256, half * 256 + 256)
        ss = _dot(hi[:, sl], ones) + _dot(lo[:, sl], ones)
        tn = t[:, sl] * lax.rsqrt(ss * (1.0 / HEAD_DIM) + NORM_EPS)
        for j in range(2):
            th = tn[:, j * HEAD_LANES:(j + 1) * HEAD_LANES] * gain
            rot = jnp.where(first_half, pltpu.roll(th, HEAD_LANES - HEAD_DIM // 2, 1),
                            pltpu.roll(th, HEAD_DIM // 2, 1))
            outs.append(((th * cosf + rot * sinf) * scale).astype(BF16))
    return jnp.concatenate(outs, axis=1)


def _inproj_kernel(x_ref, mod_ref, g_ref, w_ref, qg_ref, kg_ref, ones_ref, cos_ref, sin_ref,
                   ua_ref, q_ref, k_ref, v_ref, uc_ref):
    x = x_ref[0]
    m = mod_ref[0]
    h = _rms(x, g_ref[...]) * (1.0 + m[1:2]) + m[0:1]
    z = _dot(h.astype(BF16), w_ref[...])
    ua_ref[0] = z[:, 0:256]
    uc_ref[...] = z[:, 1792:2048]
    v_ref[0] = z[:, 1280:1792].astype(BF16)
    cosf = cos_ref[0]
    sinf = sin_ref[0]
    ones = ones_ref[...]
    q_ref[0] = _qk_prep(z[:, 256:768], qg_ref[...], ones, cosf, sinf, HEAD_DIM ** -0.5 * LOG2E)
    k_ref[0] = _qk_prep(z[:, 768:1280], kg_ref[...], ones, cosf, sinf, 1.0)


def _inproj(x, mod, g, w_in, qg, kg, ones, cosf, sinf):
    nb, s, d = x.shape
    ts = min(SEQ_TILE, s)
    full = lambda shape: pl.BlockSpec(shape, lambda b, t: (0,) * len(shape))
    return pl.pallas_call(
        _inproj_kernel,
        grid=(nb, s // ts),
        in_specs=[
            pl.BlockSpec((1, ts, d), lambda b, t: (b, t, 0)),
            pl.BlockSpec((1, 6, d), lambda b, t: (b, 0, 0)),
            full((1, d)),
            full(w_in.shape),
            full((1, HEAD_LANES)),
            full((1, HEAD_LANES)),
            full((256, 256)),
            pl.BlockSpec((1, ts, HEAD_LANES), lambda b, t: (b, t, 0)),
            pl.BlockSpec((1, ts, HEAD_LANES), lambda b, t: (b, t, 0)),
        ],
        out_specs=[
            pl.BlockSpec((1, ts, 256), lambda b, t: (b, t, 0)),
            pl.BlockSpec((1, ts, 512), lambda b, t: (b, t, 0)),
            pl.BlockSpec((1, ts, 512), lambda b, t: (b, t, 0)),
            pl.BlockSpec((1, ts, 512), lambda b, t: (b, t, 0)),
            pl.BlockSpec((ts, 256), lambda b, t: (t, b)),
        ],
        out_shape=[
            jax.ShapeDtypeStruct((nb, s, 256), F32),
            jax.ShapeDtypeStruct((nb, s, 512), BF16),
            jax.ShapeDtypeStruct((nb, s, 512), BF16),
            jax.ShapeDtypeStruct((nb, s, 512), BF16),
            jax.ShapeDtypeStruct((s, nb * 256), F32),
        ],
        compiler_params=_cparams(("parallel", "parallel")),
        name="inproj",
    )(x, mod, g, w_in, qg, kg, ones, cosf, sinf)


def _attn_kernel(qi_tab, ki_tab, par_ref, q_ref, k_ref, v_ref, sg_ref, o_ref,
                 q2_ref, m_ref, l_ref, acc_ref, *, tq, out_scale):
    p = pl.program_id(2)
    qi = qi_tab[p]
    ki = ki_tab[p]

    @pl.when(ki == 0)
    def _():
        q = q_ref[0]
        lane = lax.broadcasted_iota(jnp.int32, q.shape, 1)
        zero = jnp.zeros_like(q)
        q2_ref[0:tq, :] = jnp.where(lane < HEAD_DIM, q, zero)
        q2_ref[tq:2 * tq, :] = jnp.where(lane >= HEAD_DIM, q, zero)
        m_ref[...] = jnp.full(m_ref.shape, NEG_BIG, F32)
        l_ref[...] = jnp.zeros(l_ref.shape, F32)
        acc_ref[...] = jnp.zeros(acc_ref.shape, F32)

    def scores(masked):
        s = lax.dot_general(q2_ref[...], k_ref[0], (((1,), (1,)), ((), ())),
                            preferred_element_type=F32)
        if masked:
            row = lax.broadcasted_iota(jnp.int32, s.shape, 0) % tq
            col = lax.broadcasted_iota(jnp.int32, s.shape, 1)
            s = jnp.where((col // CHUNK) <= (row // CHUNK), s, NEG_BIG)
        return s

    def lane_partial(pr):
        tot = pr[:, 0:LANES]
        for j in range(1, pr.shape[1] // LANES):
            tot = tot + pr[:, j * LANES:(j + 1) * LANES]
        return tot

    def bounded_step(masked):
        pr = jnp.exp2(scores(masked))
        l_ref[...] += lane_partial(pr)
        acc_ref[...] += _dot(pr.astype(BF16), v_ref[0])

    def general_step(masked):
        s = scores(masked)
        m_old = m_ref[...]
        m_new = jnp.maximum(m_old, jnp.max(s, axis=1, keepdims=True))
        alpha = jnp.exp2(m_old - m_new)
        pr = jnp.exp2(s - m_new)
        l_ref[...] = alpha * l_ref[...] + lane_partial(pr)
        acc_ref[...] = alpha * acc_ref[...] + _dot(pr.astype(BF16), v_ref[0])
        m_ref[...] = m_new

    bounded = par_ref[1] > 0.5
    for masked, on_block in ((False, ki < qi), (True, ki == qi)):
        pl.when(on_block & bounded)(functools.partial(bounded_step, masked))
        pl.when(on_block & jnp.logical_not(bounded))(functools.partial(general_step, masked))

    @pl.when(ki == qi)
    def _():
        acc = acc_ref[...]
        l = jnp.sum(l_ref[...], axis=1, keepdims=True)
        o = acc[0:tq] / l[0:tq] - par_ref[0] * (acc[tq:2 * tq] / l[tq:2 * tq])
        o_ref[0] = (_rms(o, sg_ref[...]) * out_scale).astype(BF16)


def _attention(q, k, v, par, subln_g, lambda_init):
    nb, s, _ = q.shape
    tq = min(ATTN_TILE, s)
    nq = s // tq
    pairs = [(i, j) for i in range(nq) for j in range(i + 1)]
    qi_tab = jnp.asarray([a for a, _ in pairs], jnp.int32)
    ki_tab = jnp.asarray([b for _, b in pairs], jnp.int32)
    grid_spec = pltpu.PrefetchScalarGridSpec(
        num_scalar_prefetch=2,
        grid=(nb, N_HEADS, len(pairs)),
        in_specs=[
            pl.BlockSpec(memory_space=pltpu.SMEM),
            pl.BlockSpec((1, tq, HEAD_LANES), lambda b, h, p, qt, kt: (b, qt[p], h)),
            pl.BlockSpec((1, tq, HEAD_LANES), lambda b, h, p, qt, kt: (b, kt[p], h)),
            pl.BlockSpec((1, tq, HEAD_LANES), lambda b, h, p, qt, kt: (b, kt[p], h)),
            pl.BlockSpec((1, HEAD_LANES), lambda b, h, p, qt, kt: (0, 0)),
        ],
        out_specs=pl.BlockSpec((1, tq, HEAD_LANES), lambda b, h, p, qt, kt: (b, qt[p], h)),
        scratch_shapes=[
            pltpu.VMEM((2 * tq, HEAD_LANES), BF16),
            pltpu.VMEM((2 * tq, 1), F32),
            pltpu.VMEM((2 * tq, LANES), F32),
            pltpu.VMEM((2 * tq, HEAD_LANES), F32),
        ],
    )
    return pl.pallas_call(
        functools.partial(_attn_kernel, tq=tq, out_scale=1.0 - lambda_init),
        grid_spec=grid_spec,
        out_shape=jax.ShapeDtypeStruct((nb, s, N_HEADS * HEAD_LANES), BF16),
        compiler_params=_cparams(("parallel", "parallel", "arbitrary")),
        name="diff_attn",
    )(qi_tab, ki_tab, par, q, k, v, subln_g)


def _gelu_tanh(x):
    return 0.5 * x * (1.0 + jnp.tanh(math.sqrt(2.0 / math.pi) * (x + 0.044715 * (x * x * x))))


def _ssm_kernel(u_ref, bb_ref, a_ref, ct_ref, dskip_ref, gw_ref, gb_ref, y_ref,
                utm_ref, bur_ref, bui_ref, ytm_ref, xr_ref, xi_ref, *, tt, nb, width, nstate):
    @pl.when(pl.program_id(0) == 0)
    def _():
        xr_ref[...] = jnp.zeros(xr_ref.shape, F32)
        xi_ref[...] = jnp.zeros(xi_ref.shape, F32)

    nlt = width // LANES
    for b in range(nb):
        for j in range(nlt):
            utm_ref[j, pl.ds(b, tt, stride=nb), :] = u_ref[:, b * width + j * LANES:b * width + (j + 1) * LANES]
    u_tm = jnp.concatenate([utm_ref[j] for j in range(nlt)], axis=1)
    bu = _dot(u_tm.astype(BF16), bb_ref[...])
    bur_ref[...] = bu[:, :nstate]
    bui_ref[...] = bu[:, nstate:]

    ar = jnp.broadcast_to(a_ref[0:1, :], (nb, nstate))
    ai = jnp.broadcast_to(a_ref[1:2, :], (nb, nstate))

    def body(t, carry):
        xr, xi = carry
        r0 = pl.multiple_of(t * nb, nb)
        nxr = ar * xr - ai * xi + bur_ref[pl.ds(r0, nb), :]
        nxi = ar * xi + ai * xr + bui_ref[pl.ds(r0, nb), :]
        bur_ref[pl.ds(r0, nb), :] = nxr
        bui_ref[pl.ds(r0, nb), :] = nxi
        return nxr, nxi

    xr, xi = lax.fori_loop(0, tt, body, (xr_ref[...], xi_ref[...]), unroll=8)
    xr_ref[...] = xr
    xi_ref[...] = xi

    y = (_dot(bur_ref[...].astype(BF16), ct_ref[0:nstate, :])
         + _dot(bui_ref[...].astype(BF16), ct_ref[nstate:2 * nstate, :]))
    y = _gelu_tanh(y + dskip_ref[...] * u_tm)
    y = y * jax.nn.sigmoid(_dot(y.astype(BF16), gw_ref[...]) + gb_ref[...])
    for j in range(nlt):
        ytm_ref[j] = y[:, j * LANES:(j + 1) * LANES]
    for b in range(nb):
        for j in range(nlt):
            y_ref[:, b * width + j * LANES:b * width + (j + 1) * LANES] = ytm_ref[j, pl.ds(b, tt, stride=nb), :]


def _ssm(u_tm, bb, a, ct, dskip, gw, gb, nb):
    s = u_tm.shape[0]
    width = u_tm.shape[1] // nb
    nstate = a.shape[1]
    tt = min(SSM_TILE, s)
    full = lambda arr: pl.BlockSpec(arr.shape, lambda c: (0,) * arr.ndim)
    return pl.pallas_call(
        functools.partial(_ssm_kernel, tt=tt, nb=nb, width=width, nstate=nstate),
        grid=(s // tt,),
        in_specs=[pl.BlockSpec((tt, nb * width), lambda c: (c, 0)),
                  full(bb), full(a), full(ct), full(dskip), full(gw), full(gb)],
        out_specs=pl.BlockSpec((tt, nb * width), lambda c: (c, 0)),
        out_shape=jax.ShapeDtypeStruct(u_tm.shape, F32),
        scratch_shapes=[
            pltpu.VMEM((width // LANES, tt * nb, LANES), F32),
            pltpu.VMEM((tt * nb, nstate), F32),
            pltpu.VMEM((tt * nb, nstate), F32),
            pltpu.VMEM((width // LANES, tt * nb, LANES), F32),
            pltpu.VMEM((nb, nstate), F32),
            pltpu.VMEM((nb, nstate), F32),
        ],
        compiler_params=_cparams(("arbitrary",)),
        name="ssm",
    )(u_tm, bb, a, ct, dskip, gw, gb)


def _route(logits):
    lane = lax.broadcasted_iota(jnp.int32, logits.shape, 1)
    far = jnp.int32(4 * ROUTE_LANES)
    gl = jnp.where(lane < MOE_GROUPS, logits, NEG_BIG)
    gmax = jnp.max(gl, axis=1, keepdims=True)
    gidx = jnp.min(jnp.where(gl == gmax, lane, far), axis=1, keepdims=True)
    pg_top = 1.0 / jnp.sum(jnp.exp(gl - gmax), axis=1, keepdims=True)
    in_group = (lane >= MOE_GROUPS) & (((lane - MOE_GROUPS) // EXPERTS_PER_GROUP) == gidx) \
        & (lane < MOE_GROUPS + N_EXPERTS)
    el = jnp.where(in_group, logits, NEG_BIG)
    m1 = jnp.max(el, axis=1, keepdims=True)
    i1 = jnp.min(jnp.where(el == m1, lane, far), axis=1, keepdims=True)
    el2 = jnp.where(lane == i1, NEG_BIG, el)
    m2 = jnp.max(el2, axis=1, keepdims=True)
    i2 = jnp.min(jnp.where(el2 == m2, lane, far), axis=1, keepdims=True)
    e2 = jnp.exp(m2 - m1)
    w1 = pg_top / (1.0 + e2)
    w2 = pg_top * e2 / (1.0 + e2)
    idx = jnp.where(lane == 0, i1 - MOE_GROUPS, jnp.where(lane == 1, i2 - MOE_GROUPS, 0))
    wts = jnp.where(lane == 0, w1, jnp.where(lane == 1, w2, 0.0))
    return idx, wts


def _merge_kernel(x_ref, mod_ref, ua_ref, o_ref, yc_ref, g1_ref, gw_ref, gb_ref, pw_ref, ps_ref,
                  pa_ref, pb_ref, pc_ref, wo_ref, g2_ref, rw_ref, rb_ref,
                  x1_ref, h2_ref, ri_ref, rwt_ref, ubuf_ref, *, ts, d):
    ti = pl.program_id(1)
    x = x_ref[0]
    m = mod_ref[0]
    h = _rms(x, g1_ref[...]) * (1.0 + m[1:2]) + m[0:1]
    gates = jax.nn.sigmoid(_dot(h.astype(BF16), gw_ref[...]) + gb_ref[...])

    @pl.when(ti == 0)
    def _():
        ubuf_ref[0:POOL_HIST, :] = jnp.zeros((POOL_HIST, ubuf_ref.shape[1]), F32)

    u = ua_ref[0]
    ubuf_ref[POOL_HIST:POOL_HIST + ts, :] = u
    t_abs = (ti * ts + lax.broadcasted_iota(jnp.int32, (ts, 1), 0) + 1).astype(F32)
    lane = lax.broadcasted_iota(jnp.int32, u.shape, 1)
    grp = lane // (u.shape[1] // len(POOL_WINDOWS))
    run = jnp.zeros_like(u)
    pooled = jnp.zeros_like(u)
    for j in range(POOL_HIST):
        run = run + ubuf_ref[pl.ds(POOL_HIST - j, ts), :]
        if (j + 1) in POOL_WINDOWS:
            g = POOL_WINDOWS.index(j + 1)
            val = run / jnp.minimum(t_abs, float(j + 1)) - u
            pooled = jnp.where(grp == g, val, pooled)
    ubuf_ref[0:POOL_HIST, :] = ubuf_ref[ts:ts + POOL_HIST, :]
    ya = _dot(pooled.astype(BF16), pw_ref[...]) * ps_ref[...]

    mixed = (gates[:, 0:d] * _dot(ya.astype(BF16), pa_ref[...])
             + gates[:, d:2 * d] * _dot(o_ref[0], pb_ref[...])
             + gates[:, 2 * d:3 * d] * _dot(yc_ref[...].astype(BF16), pc_ref[...]))
    x1 = x + m[2:3] * _dot(mixed.astype(BF16), wo_ref[...])
    x1_ref[0] = x1
    h2 = _rms(x1, g2_ref[...]) * (1.0 + m[4:5]) + m[3:4]
    for j in range(d // LANES):
        h2_ref[pl.ds(j, ts, stride=d // LANES), :] = h2[:, j * LANES:(j + 1) * LANES]
    logits = _dot(h2.astype(BF16), rw_ref[...]) + rb_ref[...]
    idx, wts = _route(logits)
    ri_ref[0] = idx
    rwt_ref[0] = wts


def _merge(x, mod, ua, o, yc_tm, g1, gw, gb, pw, ps, pa, pb, pc, wo, g2, rw, rb):
    nb, s, d = x.shape
    ts = min(SEQ_TILE, s)
    full = lambda arr: pl.BlockSpec(arr.shape, lambda b, t: (0,) * arr.ndim)
    tile = lambda w: pl.BlockSpec((1, ts, w), lambda b, t: (b, t, 0))
    return pl.pallas_call(
        functools.partial(_merge_kernel, ts=ts, d=d),
        grid=(nb, s // ts),
        in_specs=[tile(d), pl.BlockSpec((1, 6, d), lambda b, t: (b, 0, 0)), tile(256), tile(512),
                  pl.BlockSpec((ts, 256), lambda b, t: (t, b)),
                  full(g1), full(gw), full(gb), full(pw), full(ps), full(pa), full(pb), full(pc),
                  full(wo), full(g2), full(rw), full(rb)],
        out_specs=[tile(d), pl.BlockSpec((ts * (d // LANES), LANES), lambda b, t: (b * (s // ts) + t, 0)),
                   tile(ROUTE_LANES), tile(ROUTE_LANES)],
        out_shape=[
            jax.ShapeDtypeStruct((nb, s, d), F32),
            jax.ShapeDtypeStruct((nb * s * (d // LANES), LANES), F32),
            jax.ShapeDtypeStruct((nb, s, ROUTE_LANES), jnp.int32),
            jax.ShapeDtypeStruct((nb, s, ROUTE_LANES), F32),
        ],
        scratch_shapes=[pltpu.VMEM((POOL_HIST + ts, 256), F32)],
        compiler_params=_cparams(("arbitrary", "arbitrary")),
        name="merge_route",
    )(x, mod, ua, o, yc_tm, g1, gw, gb, pw, ps, pa, pb, pc, wo, g2, rw, rb)


def _expert_kernel(blk_e, first, nused, tok_ref, tokn_ref, dst_ref, w_ref, h_hbm, w1_ref, w3_ref, w2_ref, y_hbm,
                   xbuf, ybuf, w1b, w3b, w2b, gsem, ssem, *, tmb, npc):
    i = pl.program_id(0)
    n = nused[0]
    slot = i % 2
    other = 1 - slot
    rows = tmb * npc

    def base(s):
        return pl.multiple_of(s * rows, rows)

    def gather_start(idx_ref, s):
        for r in range(tmb):
            pltpu.make_async_copy(h_hbm.at[pl.ds(pl.multiple_of(idx_ref[r], npc), npc), :],
                                  xbuf.at[pl.ds(base(s) + r * npc, npc), :], gsem.at[s]).start()

    def gather_wait(s):
        pltpu.make_async_copy(h_hbm.at[pl.ds(0, rows), :], xbuf.at[pl.ds(base(s), rows), :], gsem.at[s]).wait()

    def scatter_start(s):
        for r in range(tmb):
            pltpu.make_async_copy(ybuf.at[pl.ds(base(s) + r * npc, npc), :],
                                  y_hbm.at[pl.ds(pl.multiple_of(dst_ref[r], npc), npc), :], ssem.at[s]).start()

    def scatter_wait(s):
        pltpu.make_async_copy(ybuf.at[pl.ds(base(s), rows), :], y_hbm.at[pl.ds(0, rows), :], ssem.at[s]).wait()

    @pl.when(i == 0)
    def _():
        gather_start(tok_ref, 0)
        ybuf[...] = jnp.zeros(ybuf.shape, F32)
        dump = pltpu.make_async_copy(ybuf, y_hbm.at[pl.ds(y_hbm.shape[0] - 2 * rows, 2 * rows), :], ssem.at[0])
        dump.start()
        dump.wait()

    @pl.when((i < n) & (first[i] == 1))
    def _():
        w1b[...] = w1_ref[0].astype(BF16)
        w3b[...] = w3_ref[0].astype(BF16)
        w2b[...] = w2_ref[0].astype(BF16)

    @pl.when((i < n) & (i >= 2))
    def _():
        scatter_wait(slot)

    @pl.when(i < n)
    def _():
        gather_wait(slot)
        gather_start(tokn_ref, other)
        xb = jnp.concatenate([xbuf[pl.ds(base(slot) + j, tmb, stride=npc), :] for j in range(npc)],
                             axis=1).astype(BF16)
        a = _dot(xb, w1b[...])
        g = _dot(xb, w3b[...])
        mid = (a * jax.nn.sigmoid(a)) * g
        y = _dot(mid.astype(BF16), w2b[...]) * w_ref[...]
        for j in range(npc):
            ybuf[pl.ds(base(slot) + j, tmb, stride=npc), :] = y[:, j * LANES:(j + 1) * LANES]
        scatter_start(slot)

    @pl.when(i == n - 1)
    def _():
        gather_wait(other)
        scatter_wait(slot)

        @pl.when(n >= 2)
        def _():
            scatter_wait(other)


def _experts(h2, tok, dst, wslot, blk_e, first, nused, w1, w3, w2):
    d = w1.shape[1]
    npc = d // LANES
    t = h2.shape[0] // npc
    nslot = tok.shape[0]
    tmb = MOE_TILE
    nblk = nslot // tmb
    ff = w1.shape[2]
    smem_blk = lambda f: pl.BlockSpec((tmb,), f, memory_space=pltpu.SMEM)
    grid_spec = pltpu.PrefetchScalarGridSpec(
        num_scalar_prefetch=3,
        grid=(nblk,),
        in_specs=[
            smem_blk(lambda i, be, fi, nu: (i,)),
            smem_blk(lambda i, be, fi, nu: (jnp.minimum(i + 1, nblk - 1),)),
            smem_blk(lambda i, be, fi, nu: (i,)),
            pl.BlockSpec((tmb, 1), lambda i, be, fi, nu: (i, 0)),
            pl.BlockSpec(memory_space=pl.ANY),
            pl.BlockSpec((1, d, ff), lambda i, be, fi, nu: (be[i], 0, 0)),
            pl.BlockSpec((1, d, ff), lambda i, be, fi, nu: (be[i], 0, 0)),
            pl.BlockSpec((1, ff, d), lambda i, be, fi, nu: (be[i], 0, 0)),
        ],
        out_specs=pl.BlockSpec(memory_space=pl.ANY),
        scratch_shapes=[
            pltpu.VMEM((2 * tmb * npc, LANES), F32),
            pltpu.VMEM((2 * tmb * npc, LANES), F32),
            pltpu.VMEM((d, ff), BF16),
            pltpu.VMEM((d, ff), BF16),
            pltpu.VMEM((ff, d), BF16),
            pltpu.SemaphoreType.DMA((2,)),
            pltpu.SemaphoreType.DMA((2,)),
        ],
    )
    return pl.pallas_call(
        functools.partial(_expert_kernel, tmb=tmb, npc=npc),
        grid_spec=grid_spec,
        out_shape=jax.ShapeDtypeStruct(((2 * t + 2 * tmb) * npc, LANES), F32),
        compiler_params=_cparams(("arbitrary",)),
        name="experts",
    )(blk_e, first, nused, tok, tok, dst, wslot.reshape(nslot, 1), h2, w1, w3, w2)


def _combine_kernel(x1_ref, mod_ref, y_ref, o_ref, *, tcb, npc):
    moe = jnp.concatenate([y_ref[pl.ds(j, tcb, stride=2 * npc), :] + y_ref[pl.ds(npc + j, tcb, stride=2 * npc), :]
                           for j in range(npc)], axis=1)
    o_ref[...] = x1_ref[...] + mod_ref[0][5:6] * moe


def _combine(x1, mod, y2, seq):
    t, d = x1.shape
    npc = d // LANES
    tcb = min(COMB_TILE, seq)
    per_b = seq // tcb
    return pl.pallas_call(
        functools.partial(_combine_kernel, tcb=tcb, npc=npc),
        grid=(t // tcb,),
        in_specs=[
            pl.BlockSpec((tcb, d), lambda i: (i, 0)),
            pl.BlockSpec((1, 6, d), lambda i: (i // per_b, 0, 0)),
            pl.BlockSpec((2 * tcb * npc, LANES), lambda i: (i, 0)),
        ],
        out_specs=pl.BlockSpec((tcb, d), lambda i: (i, 0)),
        out_shape=jax.ShapeDtypeStruct((t, d), F32),
        compiler_params=_cparams(("parallel",)),
        name="combine",
    )(x1, mod, y2)


def _dispatch_plan(e_idx, wts, tmb, row_scale):
    t = e_idx.shape[0]
    e_flat = e_idx.reshape(-1)
    onehot = (e_flat[:, None] == jnp.arange(N_EXPERTS, dtype=jnp.int32)[None, :]).astype(jnp.int32)
    csum = jnp.cumsum(onehot, axis=0)
    rank = jnp.take_along_axis(csum, e_flat[:, None], axis=1)[:, 0] - 1
    counts = csum[-1]
    padded = (counts + tmb - 1) // tmb * tmb
    pend = jnp.cumsum(padded)
    pstart = pend - padded
    dest = pstart[e_flat] + rank
    nslot = 2 * t + N_EXPERTS * tmb
    src = jnp.full((nslot,), -1, jnp.int32).at[dest].set(jnp.arange(2 * t, dtype=jnp.int32))
    real = src >= 0
    pos = jnp.arange(nslot, dtype=jnp.int32)
    tok = jnp.where(real, src // 2, 0)
    dst = jnp.where(real, src, 2 * t + pos % (2 * tmb))
    wslot = jnp.where(real, wts.reshape(-1)[jnp.maximum(src, 0)], 0.0)
    blk_start = jnp.arange(nslot // tmb, dtype=jnp.int32) * tmb
    blk_e = jnp.minimum(jnp.sum(pend[None, :] <= blk_start[:, None], axis=1), N_EXPERTS - 1).astype(jnp.int32)
    first = jnp.concatenate([jnp.ones((1,), jnp.int32), (blk_e[1:] != blk_e[:-1]).astype(jnp.int32)])
    nused = (pend[-1] // tmb).astype(jnp.int32).reshape(1)
    return tok * row_scale, dst * row_scale, wslot, blk_e, first, nused


def _block_diag(blocks):
    g, r, c = blocks.shape
    eye = jnp.eye(g, dtype=blocks.dtype)
    return jnp.einsum('grc,gh->grhc', blocks, eye).reshape(g * r, g * c)


def _ssm_params(lam_re, lam_im, log_dt, b_re, b_im, c_re, c_im):
    dt = jnp.exp(log_dt)[:, None]
    mag = jnp.exp(lam_re * dt)
    ar, ai = mag * jnp.cos(lam_im * dt), mag * jnp.sin(lam_im * dt)
    nr, ni = ar - 1.0, ai
    den = lam_re * lam_re + lam_im * lam_im
    kr, ki = (nr * lam_re + ni * lam_im) / den, (ni * lam_re - nr * lam_im) / den
    bbr = kr[..., None] * b_re - ki[..., None] * b_im
    bbi = kr[..., None] * b_im + ki[..., None] * b_re
    bb = jnp.concatenate([_block_diag(jnp.swapaxes(bbr, 1, 2)), _block_diag(jnp.swapaxes(bbi, 1, 2))], axis=1)
    ct = jnp.concatenate([_block_diag(jnp.swapaxes(c_re, 1, 2)), -_block_diag(jnp.swapaxes(c_im, 1, 2))], axis=0)
    a = jnp.stack([ar.reshape(-1), ai.reshape(-1)], axis=0)
    return bb.astype(BF16), a, ct.astype(BF16)


def _rope_tables(positions):
    inv = ROPE_THETA ** (-jnp.arange(0, HEAD_DIM, 2, dtype=F32) / HEAD_DIM)
    ang = positions.astype(F32)[..., None] * inv
    cos, sin = jnp.cos(ang), jnp.sin(ang)
    cosf = jnp.concatenate([cos, cos, cos, cos], axis=-1)
    sinf = jnp.concatenate([-sin, sin, -sin, sin], axis=-1)
    return cosf, sinf


def kernel(x, c, positions, ada_w, ada_b, norm1_g, w_in, pool_w, pool_scale, q_norm_g, k_norm_g, lambda_q1, lambda_k1, lambda_q2, lambda_k2, subln_g, lam_re, lam_im, log_dt, b_re, b_im, c_re, c_im, d_skip, glu_w, glu_b, proj_a, proj_b, proj_c, gate_w, gate_b, w_out, norm2_g, router_g_w, router_g_b, router_e_w, router_e_b, moe_w1, moe_w3, moe_w2):
    nb, s, d = x.shape
    depth = ada_w.shape[0]
    cosf, sinf = _rope_tables(positions)
    mod_all = _ada_mod(c, ada_w, ada_b).reshape(depth, nb, 6, d)
    ones = _block_diag(jnp.ones((256 // HEAD_DIM, HEAD_DIM, HEAD_DIM), F32)).astype(BF16)
    row = lambda v: v.reshape(1, -1)
    tile2 = lambda v: jnp.concatenate([v, v]).reshape(1, -1)
    for l in range(depth):
        lambda_init = 0.8 - 0.6 * math.exp(-0.3 * l)
        mod = mod_all[l]
        ua, q, k, v, uc_tm = _inproj(x, mod, row(norm1_g[l]), w_in[l].astype(BF16),
                                     tile2(q_norm_g[l]), tile2(k_norm_g[l]), ones, cosf, sinf)
        lam = (jnp.exp(jnp.sum(lambda_q1[l] * lambda_k1[l])) - jnp.exp(jnp.sum(lambda_q2[l] * lambda_k2[l]))
               + lambda_init).astype(F32)
        score_bound = 1.02 * LOG2E * math.sqrt(HEAD_DIM) * jnp.max(jnp.abs(q_norm_g[l])) * jnp.max(jnp.abs(k_norm_g[l]))
        par = jnp.stack([lam, (score_bound <= SHIFT_FREE_LIMIT).astype(F32)])
        o = _attention(q, k, v, par, row(subln_g[l]), lambda_init)
        bb, a, ct = _ssm_params(lam_re[l], lam_im[l], log_dt[l], b_re[l], b_im[l], c_re[l], c_im[l])
        yc_tm = _ssm(uc_tm, bb, a, ct, row(d_skip[l]), glu_w[l].astype(BF16), row(glu_b[l]), nb)
        rw = jnp.zeros((d, ROUTE_LANES), F32).at[:, 0:MOE_GROUPS].set(router_g_w[l]) \
            .at[:, MOE_GROUPS:MOE_GROUPS + N_EXPERTS].set(router_e_w[l]).astype(BF16)
        rb = jnp.zeros((1, ROUTE_LANES), F32).at[0, 0:MOE_GROUPS].set(router_g_b[l]) \
            .at[0, MOE_GROUPS:MOE_GROUPS + N_EXPERTS].set(router_e_b[l])
        x1, h2, ri, rwt = _merge(x, mod, ua, o, yc_tm, row(norm1_g[l]), gate_w[l].astype(BF16), row(gate_b[l]),
                                 _block_diag(pool_w[l]).astype(BF16), row(pool_scale[l]),
                                 proj_a[l].astype(BF16), proj_b[l].astype(BF16), proj_c[l].astype(BF16),
                                 w_out[l].astype(BF16), row(norm2_g[l]), rw, rb)
        t = nb * s
        tok, dst, wslot, blk_e, first, nused = _dispatch_plan(
            ri.reshape(t, ROUTE_LANES)[:, 0:2], rwt.reshape(t, ROUTE_LANES)[:, 0:2], MOE_TILE, d // LANES)
        y2 = _experts(h2, tok, dst, wslot, blk_e, first, nused, moe_w1[l], moe_w3[l], moe_w2[l])
        x = _combine(x1.reshape(t, d), mod, y2, s).reshape(nb, s, d)
    return x
```

```python
import functools
import math

import jax
import jax.numpy as jnp
from jax import lax
from jax.experimental import pallas as pl
from jax.experimental.pallas import tpu as pltpu

F32 = jnp.float32
BF16 = jnp.bfloat16

NORM_EPS = 1e-6
CHUNK = 64
POOL_WINDOWS = (2, 4, 8, 16)
POOL_HIST = 16
HEAD_DIM = 64
N_HEADS = 4
HEAD_LANES = 2 * HEAD_DIM
ROPE_THETA = 10000.0
SSM_GROUP_DIM = 16
SSM_STATE = 64
MOE_GROUPS = 4
EXPERTS_PER_GROUP = 8
N_EXPERTS = MOE_GROUPS * EXPERTS_PER_GROUP
ROUTE_LANES = 128
NEG_BIG = -1e30
LANES = 128
LOG2E = math.log2(math.e)
SHIFT_FREE_LIMIT = 60.0

VMEM_LIMIT = 48 * 1024 * 1024

SEQ_TILE = 512
ATTN_TILE = 512
SSM_TILE = 256
MOE_TILE = 512
COMB_TILE = 256


def _cparams(sem):
    return pltpu.CompilerParams(dimension_semantics=sem, vmem_limit_bytes=VMEM_LIMIT)


def _rms(x, g):
    return x * lax.rsqrt(jnp.mean(x * x, axis=-1, keepdims=True) + NORM_EPS) * g


def _dot(a, b):
    return jnp.dot(a, b, preferred_element_type=F32)


def _ada_kernel(c_ref, w_ref, b_ref, o_ref):
    c = c_ref[...]
    act = (c * jax.nn.sigmoid(c)).astype(BF16)
    o_ref[0] = _dot(act, w_ref[0].astype(BF16)) + b_ref[0]


def _ada_mod(c, ada_w, ada_b):
    depth, d, n = ada_w.shape
    nb = c.shape[0]
    tn = 1536
    return pl.pallas_call(
        _ada_kernel,
        grid=(depth, n // tn),
        in_specs=[
            pl.BlockSpec((nb, d), lambda l, j: (0, 0)),
            pl.BlockSpec((1, d, tn), lambda l, j: (l, 0, j)),
            pl.BlockSpec((1, 1, tn), lambda l, j: (l, 0, j)),
        ],
        out_specs=pl.BlockSpec((1, nb, tn), lambda l, j: (l, 0, j)),
        out_shape=jax.ShapeDtypeStruct((depth, nb, n), F32),
        compiler_params=_cparams(("arbitrary", "arbitrary")),
        name="ada_mod",
    )(c, ada_w, ada_b.reshape(depth, 1, n))


def _qk_prep(t, gain, ones, cosf, sinf, scale):
    t2 = t * t
    hi = t2.astype(BF16)
    lo = (t2 - hi.astype(F32)).astype(BF16)
    lane = lax.broadcasted_iota(jnp.int32, (t.shape[0], HEAD_LANES), 1)
    first_half = (lane % HEAD_DIM) < (HEAD_DIM // 2)
    outs = []
    for half in range(2):
        sl = slice(half * 256, half * 256 + 256)
        ss = _dot(hi[:, sl], ones) + _dot(lo[:, sl], ones)
        tn = t[:, sl] * lax.rsqrt(ss * (1.0 / HEAD_DIM) + NORM_EPS)
        for j in range(2):
            th = tn[:, j * HEAD_LANES:(j + 1) * HEAD_LANES] * gain
            rot = jnp.where(first_half, pltpu.roll(th, HEAD_LANES - HEAD_DIM // 2, 1),
                            pltpu.roll(th, HEAD_DIM // 2, 1))
            outs.append(((th * cosf + rot * sinf) * scale).astype(BF16))
    return jnp.concatenate(outs, axis=1)


def _inproj_kernel(x_ref, mod_ref, g_ref, w_ref, qg_ref, kg_ref, ones_ref, cos_ref, sin_ref,
                   ua_ref, q_ref, k_ref, v_ref, uc_ref):
    x = x_ref[0]
    m = mod_ref[0]
    h = _rms(x, g_ref[...]) * (1.0 + m[1:2]) + m[0:1]
    z = _dot(h.astype(BF16), w_ref[...])
    ua_ref[0] = z[:, 0:256]
    uc_ref[...] = z[:, 1792:2048]
    v_ref[0] = z[:, 1280:1792].astype(BF16)
    cosf = cos_ref[0]
    sinf = sin_ref[0]
    ones = ones_ref[...]
    q_ref[0] = _qk_prep(z[:, 256:768], qg_ref[...], ones, cosf, sinf, HEAD_DIM ** -0.5 * LOG2E)
    k_ref[0] = _qk_prep(z[:, 768:1280], kg_ref[...], ones, cosf, sinf, 1.0)


def _inproj(x, mod, g, w_in, qg, kg, ones, cosf, sinf):
    nb, s, d = x.shape
    ts = min(SEQ_TILE, s)
    full = lambda shape: pl.BlockSpec(shape, lambda b, t: (0,) * len(shape))
    return pl.pallas_call(
        _inproj_kernel,
        grid=(nb, s // ts),
        in_specs=[
            pl.BlockSpec((1, ts, d), lambda b, t: (b, t, 0)),
            pl.BlockSpec((1, 6, d), lambda b, t: (b, 0, 0)),
            full((1, d)),
            full(w_in.shape),
            full((1, HEAD_LANES)),
            full((1, HEAD_LANES)),
            full((256, 256)),
            pl.BlockSpec((1, ts, HEAD_LANES), lambda b, t: (b, t, 0)),
            pl.BlockSpec((1, ts, HEAD_LANES), lambda b, t: (b, t, 0)),
        ],
        out_specs=[
            pl.BlockSpec((1, ts, 256), lambda b, t: (b, t, 0)),
            pl.BlockSpec((1, ts, 512), lambda b, t: (b, t, 0)),
            pl.BlockSpec((1, ts, 512), lambda b, t: (b, t, 0)),
            pl.BlockSpec((1, ts, 512), lambda b, t: (b, t, 0)),
            pl.BlockSpec((ts, 256), lambda b, t: (t, b)),
        ],
        out_shape=[
            jax.ShapeDtypeStruct((nb, s, 256), F32),
            jax.ShapeDtypeStruct((nb, s, 512), BF16),
            jax.ShapeDtypeStruct((nb, s, 512), BF16),
            jax.ShapeDtypeStruct((nb, s, 512), BF16),
            jax.ShapeDtypeStruct((s, nb * 256), F32),
        ],
        compiler_params=_cparams(("parallel", "parallel")),
        name="inproj",
    )(x, mod, g, w_in, qg, kg, ones, cosf, sinf)


def _attn_kernel(par_ref, q_ref, k_ref, v_ref, sg_ref, o_ref, q2_ref, m_ref, l_ref, acc_ref, *, tq, out_scale):
    qi = pl.program_id(2)
    q = q_ref[0]
    lane = lax.broadcasted_iota(jnp.int32, q.shape, 1)
    zero = jnp.zeros_like(q)
    q2_ref[0:tq, :] = jnp.where(lane < HEAD_DIM, q, zero)
    q2_ref[tq:2 * tq, :] = jnp.where(lane >= HEAD_DIM, q, zero)
    m_ref[...] = jnp.full(m_ref.shape, NEG_BIG, F32)
    l_ref[...] = jnp.zeros(l_ref.shape, F32)
    acc_ref[...] = jnp.zeros(acc_ref.shape, F32)

    def scores(ki, masked):
        k = k_ref[0, pl.ds(pl.multiple_of(ki * tq, tq), tq), :]
        s = lax.dot_general(q2_ref[...], k, (((1,), (1,)), ((), ())), preferred_element_type=F32)
        if masked:
            row = lax.broadcasted_iota(jnp.int32, s.shape, 0) % tq
            col = lax.broadcasted_iota(jnp.int32, s.shape, 1)
            s = jnp.where((col // CHUNK) <= (row // CHUNK), s, NEG_BIG)
        return s

    def values(ki):
        return v_ref[0, pl.ds(pl.multiple_of(ki * tq, tq), tq), :]

    def lane_partial(pr):
        tot = pr[:, 0:LANES]
        for j in range(1, pr.shape[1] // LANES):
            tot = tot + pr[:, j * LANES:(j + 1) * LANES]
        return tot

    def bounded_step(ki, masked):
        pr = jnp.exp2(scores(ki, masked))
        l_ref[...] += lane_partial(pr)
        acc_ref[...] += _dot(pr.astype(BF16), values(ki))

    def general_step(ki, masked):
        s = scores(ki, masked)
        m_old = m_ref[...]
        m_new = jnp.maximum(m_old, jnp.max(s, axis=1, keepdims=True))
        alpha = jnp.exp2(m_old - m_new)
        pr = jnp.exp2(s - m_new)
        l_ref[...] = alpha * l_ref[...] + lane_partial(pr)
        acc_ref[...] = alpha * acc_ref[...] + _dot(pr.astype(BF16), values(ki))
        m_ref[...] = m_new

    def sweep(step):
        def body(ki, c):
            step(ki, False)
            return c

        lax.fori_loop(0, qi, body, 0)
        step(qi, True)

    bounded = par_ref[1] > 0.5
    pl.when(bounded)(functools.partial(sweep, bounded_step))
    pl.when(jnp.logical_not(bounded))(functools.partial(sweep, general_step))

    acc = acc_ref[...]
    l = jnp.sum(l_ref[...], axis=1, keepdims=True)
    o = acc[0:tq] / l[0:tq] - par_ref[0] * (acc[tq:2 * tq] / l[tq:2 * tq])
    o_ref[0] = (_rms(o, sg_ref[...]) * out_scale).astype(BF16)


def _attention(q, k, v, par, subln_g, lambda_init):
    nb, s, _ = q.shape
    tq = min(ATTN_TILE, s)
    return pl.pallas_call(
        functools.partial(_attn_kernel, tq=tq, out_scale=1.0 - lambda_init),
        grid=(nb, N_HEADS, s // tq),
        in_specs=[
            pl.BlockSpec(memory_space=pltpu.SMEM),
            pl.BlockSpec((1, tq, HEAD_LANES), lambda b, h, i: (b, i, h)),
            pl.BlockSpec((1, s, HEAD_LANES), lambda b, h, i: (b, 0, h)),
            pl.BlockSpec((1, s, HEAD_LANES), lambda b, h, i: (b, 0, h)),
            pl.BlockSpec((1, HEAD_LANES), lambda b, h, i: (0, 0)),
        ],
        out_specs=pl.BlockSpec((1, tq, HEAD_LANES), lambda b, h, i: (b, i, h)),
        out_shape=jax.ShapeDtypeStruct((nb, s, N_HEADS * HEAD_LANES), BF16),
        scratch_shapes=[
            pltpu.VMEM((2 * tq, HEAD_LANES), BF16),
            pltpu.VMEM((2 * tq, 1), F32),
            pltpu.VMEM((2 * tq, LANES), F32),
            pltpu.VMEM((2 * tq, HEAD_LANES), F32),
        ],
        compiler_params=_cparams(("parallel", "parallel", "arbitrary")),
        name="diff_attn",
    )(par, q, k, v, subln_g)


def _gelu_tanh(x):
    return 0.5 * x * (1.0 + jnp.tanh(math.sqrt(2.0 / math.pi) * (x + 0.044715 * (x * x * x))))


def _ssm_kernel(u_ref, bb_ref, a_ref, ct_ref, dskip_ref, gw_ref, gb_ref, y_ref,
                utm_ref, bur_ref, bui_ref, ytm_ref, xr_ref, xi_ref, *, tt, nb, width, nstate):
    @pl.when(pl.program_id(0) == 0)
    def _():
        xr_ref[...] = jnp.zeros(xr_ref.shape, F32)
        xi_ref[...] = jnp.zeros(xi_ref.shape, F32)

    nlt = width // LANES
    for b in range(nb):
        for j in range(nlt):
            utm_ref[j, pl.ds(b, tt, stride=nb), :] = u_ref[:, b * width + j * LANES:b * width + (j + 1) * LANES]
    u_tm = jnp.concatenate([utm_ref[j] for j in range(nlt)], axis=1)
    bu = _dot(u_tm.astype(BF16), bb_ref[...])
    bur_ref[...] = bu[:, :nstate]
    bui_ref[...] = bu[:, nstate:]

    ar = jnp.broadcast_to(a_ref[0:1, :], (nb, nstate))
    ai = jnp.broadcast_to(a_ref[1:2, :], (nb, nstate))

    def body(t, carry):
        xr, xi = carry
        r0 = pl.multiple_of(t * nb, nb)
        nxr = ar * xr - ai * xi + bur_ref[pl.ds(r0, nb), :]
        nxi = ar * xi + ai * xr + bui_ref[pl.ds(r0, nb), :]
        bur_ref[pl.ds(r0, nb), :] = nxr
        bui_ref[pl.ds(r0, nb), :] = nxi
        return nxr, nxi

    xr, xi = lax.fori_loop(0, tt, body, (xr_ref[...], xi_ref[...]), unroll=8)
    xr_ref[...] = xr
    xi_ref[...] = xi

    y = (_dot(bur_ref[...].astype(BF16), ct_ref[0:nstate, :])
         + _dot(bui_ref[...].astype(BF16), ct_ref[nstate:2 * nstate, :]))
    y = _gelu_tanh(y + dskip_ref[...] * u_tm)
    y = y * jax.nn.sigmoid(_dot(y.astype(BF16), gw_ref[...]) + gb_ref[...])
    for j in range(nlt):
        ytm_ref[j] = y[:, j * LANES:(j + 1) * LANES]
    for b in range(nb):
        for j in range(nlt):
            y_ref[:, b * width + j * LANES:b * width + (j + 1) * LANES] = ytm_ref[j, pl.ds(b, tt, stride=nb), :]


def _ssm(u_tm, bb, a, ct, dskip, gw, gb, nb):
    s = u_tm.shape[0]
    width = u_tm.shape[1] // nb
    nstate = a.shape[1]
    tt = min(SSM_TILE, s)
    full = lambda arr: pl.BlockSpec(arr.shape, lambda c: (0,) * arr.ndim)
    return pl.pallas_call(
        functools.partial(_ssm_kernel, tt=tt, nb=nb, width=width, nstate=nstate),
        grid=(s // tt,),
        in_specs=[pl.BlockSpec((tt, nb * width), lambda c: (c, 0)),
                  full(bb), full(a), full(ct), full(dskip), full(gw), full(gb)],
        out_specs=pl.BlockSpec((tt, nb * width), lambda c: (c, 0)),
        out_shape=jax.ShapeDtypeStruct(u_tm.shape, F32),
        scratch_shapes=[
            pltpu.VMEM((width // LANES, tt * nb, LANES), F32),
            pltpu.VMEM((tt * nb, nstate), F32),
            pltpu.VMEM((tt * nb, nstate), F32),
            pltpu.VMEM((width // LANES, tt * nb, LANES), F32),
            pltpu.VMEM((nb, nstate), F32),
            pltpu.VMEM((nb, nstate), F32),
        ],
        compiler_params=_cparams(("arbitrary",)),
        name="ssm",
    )(u_tm, bb, a, ct, dskip, gw, gb)


def _route(logits):
    lane = lax.broadcasted_iota(jnp.int32, logits.shape, 1)
    far = jnp.int32(4 * ROUTE_LANES)
    gl = jnp.where(lane < MOE_GROUPS, logits, NEG_BIG)
    gmax = jnp.max(gl, axis=1, keepdims=True)
    gidx = jnp.min(jnp.where(gl == gmax, lane, far), axis=1, keepdims=True)
    pg_top = 1.0 / jnp.sum(jnp.exp(gl - gmax), axis=1, keepdims=True)
    in_group = (lane >= MOE_GROUPS) & (((lane - MOE_GROUPS) // EXPERTS_PER_GROUP) == gidx) \
        & (lane < MOE_GROUPS + N_EXPERTS)
    el = jnp.where(in_group, logits, NEG_BIG)
    m1 = jnp.max(el, axis=1, keepdims=True)
    i1 = jnp.min(jnp.where(el == m1, lane, far), axis=1, keepdims=True)
    el2 = jnp.where(lane == i1, NEG_BIG, el)
    m2 = jnp.max(el2, axis=1, keepdims=True)
    i2 = jnp.min(jnp.where(el2 == m2, lane, far), axis=1, keepdims=True)
    e2 = jnp.exp(m2 - m1)
    w1 = pg_top / (1.0 + e2)
    w2 = pg_top * e2 / (1.0 + e2)
    idx = jnp.where(lane == 0, i1 - MOE_GROUPS, jnp.where(lane == 1, i2 - MOE_GROUPS, 0))
    wts = jnp.where(lane == 0, w1, jnp.where(lane == 1, w2, 0.0))
    return idx, wts


def _merge_kernel(x_ref, mod_ref, ua_ref, o_ref, yc_ref, g1_ref, gw_ref, gb_ref, pw_ref, ps_ref,
                  pa_ref, pb_ref, pc_ref, wo_ref, g2_ref, rw_ref, rb_ref,
                  x1_ref, h2_ref, ri_ref, rwt_ref, ubuf_ref, *, ts, d):
    ti = pl.program_id(1)
    x = x_ref[0]
    m = mod_ref[0]
    h = _rms(x, g1_ref[...]) * (1.0 + m[1:2]) + m[0:1]
    gates = jax.nn.sigmoid(_dot(h.astype(BF16), gw_ref[...]) + gb_ref[...])

    @pl.when(ti == 0)
    def _():
        ubuf_ref[0:POOL_HIST, :] = jnp.zeros((POOL_HIST, ubuf_ref.shape[1]), F32)

    u = ua_ref[0]
    ubuf_ref[POOL_HIST:POOL_HIST + ts, :] = u
    t_abs = (ti * ts + lax.broadcasted_iota(jnp.int32, (ts, 1), 0) + 1).astype(F32)
    lane = lax.broadcasted_iota(jnp.int32, u.shape, 1)
    grp = lane // (u.shape[1] // len(POOL_WINDOWS))
    run = jnp.zeros_like(u)
    pooled = jnp.zeros_like(u)
    for j in range(POOL_HIST):
        run = run + ubuf_ref[pl.ds(POOL_HIST - j, ts), :]
        if (j + 1) in POOL_WINDOWS:
            g = POOL_WINDOWS.index(j + 1)
            val = run / jnp.minimum(t_abs, float(j + 1)) - u
            pooled = jnp.where(grp == g, val, pooled)
    ubuf_ref[0:POOL_HIST, :] = ubuf_ref[ts:ts + POOL_HIST, :]
    ya = _dot(pooled.astype(BF16), pw_ref[...]) * ps_ref[...]

    mixed = (gates[:, 0:d] * _dot(ya.astype(BF16), pa_ref[...])
             + gates[:, d:2 * d] * _dot(o_ref[0], pb_ref[...])
             + gates[:, 2 * d:3 * d] * _dot(yc_ref[...].astype(BF16), pc_ref[...]))
    x1 = x + m[2:3] * _dot(mixed.astype(BF16), wo_ref[...])
    x1_ref[0] = x1
    h2 = _rms(x1, g2_ref[...]) * (1.0 + m[4:5]) + m[3:4]
    for j in range(d // LANES):
        h2_ref[pl.ds(j, ts, stride=d // LANES), :] = h2[:, j * LANES:(j + 1) * LANES]
    logits = _dot(h2.astype(BF16), rw_ref[...]) + rb_ref[...]
    idx, wts = _route(logits)
    ri_ref[0] = idx
    rwt_ref[0] = wts


def _merge(x, mod, ua, o, yc_tm, g1, gw, gb, pw, ps, pa, pb, pc, wo, g2, rw, rb):
    nb, s, d = x.shape
    ts = min(SEQ_TILE, s)
    full = lambda arr: pl.BlockSpec(arr.shape, lambda b, t: (0,) * arr.ndim)
    tile = lambda w: pl.BlockSpec((1, ts, w), lambda b, t: (b, t, 0))
    return pl.pallas_call(
        functools.partial(_merge_kernel, ts=ts, d=d),
        grid=(nb, s // ts),
        in_specs=[tile(d), pl.BlockSpec((1, 6, d), lambda b, t: (b, 0, 0)), tile(256), tile(512),
                  pl.BlockSpec((ts, 256), lambda b, t: (t, b)),
                  full(g1), full(gw), full(gb), full(pw), full(ps), full(pa), full(pb), full(pc),
                  full(wo), full(g2), full(rw), full(rb)],
        out_specs=[tile(d), pl.BlockSpec((ts * (d // LANES), LANES), lambda b, t: (b * (s // ts) + t, 0)),
                   tile(ROUTE_LANES), tile(ROUTE_LANES)],
        out_shape=[
            jax.ShapeDtypeStruct((nb, s, d), F32),
            jax.ShapeDtypeStruct((nb * s * (d // LANES), LANES), F32),
            jax.ShapeDtypeStruct((nb, s, ROUTE_LANES), jnp.int32),
            jax.ShapeDtypeStruct((nb, s, ROUTE_LANES), F32),
        ],
        scratch_shapes=[pltpu.VMEM((POOL_HIST + ts, 256), F32)],
        compiler_params=_cparams(("arbitrary", "arbitrary")),
        name="merge_route",
    )(x, mod, ua, o, yc_tm, g1, gw, gb, pw, ps, pa, pb, pc, wo, g2, rw, rb)


def _expert_kernel(blk_e, first, nused, tok_ref, tokn_ref, dst_ref, h_hbm, w1_ref, w3_ref, w2_ref, y_hbm,
                   xbuf, ybuf, w1b, w3b, w2b, gsem, ssem, *, tmb, npc):
    i = pl.program_id(0)
    n = nused[0]
    slot = i % 2
    other = 1 - slot
    rows = tmb * npc

    def base(s):
        return pl.multiple_of(s * rows, rows)

    def gather_start(idx_ref, s):
        for r in range(tmb):
            pltpu.make_async_copy(h_hbm.at[pl.ds(pl.multiple_of(idx_ref[r], npc), npc), :],
                                  xbuf.at[pl.ds(base(s) + r * npc, npc), :], gsem.at[s]).start(priority=r % 2)

    def gather_wait(s):
        pltpu.make_async_copy(h_hbm.at[pl.ds(0, rows), :], xbuf.at[pl.ds(base(s), rows), :], gsem.at[s]).wait()

    def scatter_start(s):
        for r in range(tmb):
            pltpu.make_async_copy(ybuf.at[pl.ds(base(s) + r * npc, npc), :],
                                  y_hbm.at[pl.ds(pl.multiple_of(dst_ref[r], npc), npc), :], ssem.at[s]).start(priority=r % 2)

    def scatter_wait(s):
        pltpu.make_async_copy(ybuf.at[pl.ds(base(s), rows), :], y_hbm.at[pl.ds(0, rows), :], ssem.at[s]).wait()

    @pl.when(i == 0)
    def _():
        gather_start(tok_ref, 0)
        ybuf[...] = jnp.zeros(ybuf.shape, F32)
        dump = pltpu.make_async_copy(ybuf, y_hbm.at[pl.ds(y_hbm.shape[0] - 2 * rows, 2 * rows), :], ssem.at[0])
        dump.start()
        dump.wait()

    @pl.when((i < n) & (first[i] == 1))
    def _():
        w1b[...] = w1_ref[0, 0].astype(BF16)
        w3b[...] = w3_ref[0, 0].astype(BF16)
        w2b[...] = w2_ref[0, 0].astype(BF16)

    @pl.when((i < n) & (i >= 2))
    def _():
        scatter_wait(slot)

    @pl.when(i < n)
    def _():
        gather_wait(slot)
        gather_start(tokn_ref, other)
        xb = jnp.concatenate([xbuf[pl.ds(base(slot) + j, tmb, stride=npc), :] for j in range(npc)],
                             axis=1).astype(BF16)
        a = _dot(xb, w1b[...])
        g = _dot(xb, w3b[...])
        mid = (a * jax.nn.sigmoid(a)) * g
        y = _dot(mid.astype(BF16), w2b[...])
        for j in range(npc):
            ybuf[pl.ds(base(slot) + j, tmb, stride=npc), :] = y[:, j * LANES:(j + 1) * LANES]
        scatter_start(slot)

    @pl.when(i == n - 1)
    def _():
        gather_wait(other)
        scatter_wait(slot)

        @pl.when(n >= 2)
        def _():
            scatter_wait(other)


def _experts(h2, tok, dst, blk_e, first, nused, w1, w3, w2, layer):
    d = w1.shape[2]
    npc = d // LANES
    t = h2.shape[0] // npc
    nslot = tok.shape[0]
    tmb = MOE_TILE
    nblk = nslot // tmb
    ff = w1.shape[3]
    smem_blk = lambda f: pl.BlockSpec((tmb,), f, memory_space=pltpu.SMEM)
    grid_spec = pltpu.PrefetchScalarGridSpec(
        num_scalar_prefetch=3,
        grid=(nblk,),
        in_specs=[
            smem_blk(lambda i, be, fi, nu: (i,)),
            smem_blk(lambda i, be, fi, nu: (jnp.minimum(i + 1, nblk - 1),)),
            smem_blk(lambda i, be, fi, nu: (i,)),
            pl.BlockSpec(memory_space=pl.ANY),
            pl.BlockSpec((1, 1, d, ff), lambda i, be, fi, nu: (layer, be[i], 0, 0)),
            pl.BlockSpec((1, 1, d, ff), lambda i, be, fi, nu: (layer, be[i], 0, 0)),
            pl.BlockSpec((1, 1, ff, d), lambda i, be, fi, nu: (layer, be[i], 0, 0)),
        ],
        out_specs=pl.BlockSpec(memory_space=pl.ANY),
        scratch_shapes=[
            pltpu.VMEM((2 * tmb * npc, LANES), F32),
            pltpu.VMEM((2 * tmb * npc, LANES), F32),
            pltpu.VMEM((d, ff), BF16),
            pltpu.VMEM((d, ff), BF16),
            pltpu.VMEM((ff, d), BF16),
            pltpu.SemaphoreType.DMA((2,)),
            pltpu.SemaphoreType.DMA((2,)),
        ],
    )
    return pl.pallas_call(
        functools.partial(_expert_kernel, tmb=tmb, npc=npc),
        grid_spec=grid_spec,
        out_shape=jax.ShapeDtypeStruct(((2 * t + 2 * tmb) * npc, LANES), F32),
        compiler_params=_cparams(("arbitrary",)),
        name="experts",
    )(blk_e, first, nused, tok, tok, dst, h2, w1, w3, w2)


def _combine_kernel(x1_ref, mod_ref, w_ref, y_ref, o_ref, *, tcb, npc):
    w = w_ref[...]
    w0, w1 = w[:, 0:1], w[:, 1:2]
    moe = jnp.concatenate([w0 * y_ref[pl.ds(j, tcb, stride=2 * npc), :] + w1 * y_ref[pl.ds(npc + j, tcb, stride=2 * npc), :]
                           for j in range(npc)], axis=1)
    o_ref[...] = x1_ref[...] + mod_ref[0][5:6] * moe


def _combine(x1, mod, wts, y2, seq):
    t, d = x1.shape
    npc = d // LANES
    tcb = min(COMB_TILE, seq)
    per_b = seq // tcb
    return pl.pallas_call(
        functools.partial(_combine_kernel, tcb=tcb, npc=npc),
        grid=(t // tcb,),
        in_specs=[
            pl.BlockSpec((tcb, d), lambda i: (i, 0)),
            pl.BlockSpec((1, 6, d), lambda i: (i // per_b, 0, 0)),
            pl.BlockSpec((tcb, ROUTE_LANES), lambda i: (i, 0)),
            pl.BlockSpec((2 * tcb * npc, LANES), lambda i: (i, 0)),
        ],
        out_specs=pl.BlockSpec((tcb, d), lambda i: (i, 0)),
        out_shape=jax.ShapeDtypeStruct((t, d), F32),
        compiler_params=_cparams(("parallel",)),
        name="combine",
    )(x1, mod, wts, y2)


def _dispatch_plan(e_idx, tmb, row_scale):
    t = e_idx.shape[0]
    e_flat = e_idx.reshape(-1)
    onehot = (e_flat[:, None] == jnp.arange(N_EXPERTS, dtype=jnp.int32)[None, :]).astype(jnp.int32)
    csum = jnp.cumsum(onehot, axis=0)
    counts = csum[-1]
    padded = (counts + tmb - 1) // tmb * tmb
    pend = jnp.cumsum(padded)
    pstart = pend - padded
    dest = jnp.sum(onehot * (csum - 1 + pstart[None, :]), axis=1)
    nslot = 2 * t + N_EXPERTS * tmb
    src = jnp.full((nslot,), -1, jnp.int32).at[dest].set(jnp.arange(2 * t, dtype=jnp.int32))
    real = src >= 0
    pos = jnp.arange(nslot, dtype=jnp.int32)
    tok = jnp.where(real, src // 2, 0)
    dst = jnp.where(real, src, 2 * t + pos % (2 * tmb))
    blk_start = jnp.arange(nslot // tmb, dtype=jnp.int32) * tmb
    blk_e = jnp.minimum(jnp.sum(pend[None, :] <= blk_start[:, None], axis=1), N_EXPERTS - 1).astype(jnp.int32)
    first = jnp.concatenate([jnp.ones((1,), jnp.int32), (blk_e[1:] != blk_e[:-1]).astype(jnp.int32)])
    nused = (pend[-1] // tmb).astype(jnp.int32).reshape(1)
    return tok * row_scale, dst * row_scale, blk_e, first, nused


def _block_diag(blocks):
    g, r, c = blocks.shape
    eye = jnp.eye(g, dtype=blocks.dtype)
    return jnp.einsum('grc,gh->grhc', blocks, eye).reshape(g * r, g * c)


def _ssm_params(lam_re, lam_im, log_dt, b_re, b_im, c_re, c_im):
    dt = jnp.exp(log_dt)[:, None]
    mag = jnp.exp(lam_re * dt)
    ar, ai = mag * jnp.cos(lam_im * dt), mag * jnp.sin(lam_im * dt)
    nr, ni = ar - 1.0, ai
    den = lam_re * lam_re + lam_im * lam_im
    kr, ki = (nr * lam_re + ni * lam_im) / den, (ni * lam_re - nr * lam_im) / den
    bbr = kr[..., None] * b_re - ki[..., None] * b_im
    bbi = kr[..., None] * b_im + ki[..., None] * b_re
    bb = jnp.concatenate([_block_diag(jnp.swapaxes(bbr, 1, 2)), _block_diag(jnp.swapaxes(bbi, 1, 2))], axis=1)
    ct = jnp.concatenate([_block_diag(jnp.swapaxes(c_re, 1, 2)), -_block_diag(jnp.swapaxes(c_im, 1, 2))], axis=0)
    a = jnp.stack([ar.reshape(-1), ai.reshape(-1)], axis=0)
    return bb.astype(BF16), a, ct.astype(BF16)


def _rope_tables(positions):
    inv = ROPE_THETA ** (-jnp.arange(0, HEAD_DIM, 2, dtype=F32) / HEAD_DIM)
    ang = positions.astype(F32)[..., None] * inv
    cos, sin = jnp.cos(ang), jnp.sin(ang)
    cosf = jnp.concatenate([cos, cos, cos, cos], axis=-1)
    sinf = jnp.concatenate([-sin, sin, -sin, sin], axis=-1)
    return cosf, sinf


def kernel(x, c, positions, ada_w, ada_b, norm1_g, w_in, pool_w, pool_scale, q_norm_g, k_norm_g, lambda_q1, lambda_k1, lambda_q2, lambda_k2, subln_g, lam_re, lam_im, log_dt, b_re, b_im, c_re, c_im, d_skip, glu_w, glu_b, proj_a, proj_b, proj_c, gate_w, gate_b, w_out, norm2_g, router_g_w, router_g_b, router_e_w, router_e_b, moe_w1, moe_w3, moe_w2):
    nb, s, d = x.shape
    depth = ada_w.shape[0]
    cosf, sinf = _rope_tables(positions)
    mod_all = _ada_mod(c, ada_w, ada_b).reshape(depth, nb, 6, d)
    ones = _block_diag(jnp.ones((256 // HEAD_DIM, HEAD_DIM, HEAD_DIM), F32)).astype(BF16)
    row = lambda v: v.reshape(1, -1)
    tile2 = lambda v: jnp.concatenate([v, v]).reshape(1, -1)
    for l in range(depth):
        lambda_init = 0.8 - 0.6 * math.exp(-0.3 * l)
        mod = mod_all[l]
        ua, q, k, v, uc_tm = _inproj(x, mod, row(norm1_g[l]), w_in[l].astype(BF16),
                                     tile2(q_norm_g[l]), tile2(k_norm_g[l]), ones, cosf, sinf)
        lam = (jnp.exp(jnp.sum(lambda_q1[l] * lambda_k1[l])) - jnp.exp(jnp.sum(lambda_q2[l] * lambda_k2[l]))
               + lambda_init).astype(F32)
        score_bound = 1.02 * LOG2E * math.sqrt(HEAD_DIM) * jnp.max(jnp.abs(q_norm_g[l])) * jnp.max(jnp.abs(k_norm_g[l]))
        par = jnp.stack([lam, (score_bound <= SHIFT_FREE_LIMIT).astype(F32)])
        o = _attention(q, k, v, par, row(subln_g[l]), lambda_init)
        bb, a, ct = _ssm_params(lam_re[l], lam_im[l], log_dt[l], b_re[l], b_im[l], c_re[l], c_im[l])
        yc_tm = _ssm(uc_tm, bb, a, ct, row(d_skip[l]), glu_w[l].astype(BF16), row(glu_b[l]), nb)
        rw = jnp.zeros((d, ROUTE_LANES), F32).at[:, 0:MOE_GROUPS].set(router_g_w[l]) \
            .at[:, MOE_GROUPS:MOE_GROUPS + N_EXPERTS].set(router_e_w[l]).astype(BF16)
        rb = jnp.zeros((1, ROUTE_LANES), F32).at[0, 0:MOE_GROUPS].set(router_g_b[l]) \
            .at[0, MOE_GROUPS:MOE_GROUPS + N_EXPERTS].set(router_e_b[l])
        x1, h2, ri, rwt = _merge(x, mod, ua, o, yc_tm, row(norm1_g[l]), gate_w[l].astype(BF16), row(gate_b[l]),
                                 _block_diag(pool_w[l]).astype(BF16), row(pool_scale[l]),
                                 proj_a[l].astype(BF16), proj_b[l].astype(BF16), proj_c[l].astype(BF16),
                                 w_out[l].astype(BF16), row(norm2_g[l]), rw, rb)
        t = nb * s
        tok, dst, blk_e, first, nused = _dispatch_plan(ri.reshape(t, ROUTE_LANES)[:, 0:2], MOE_TILE, d // LANES)
        y2 = _experts(h2, tok, dst, blk_e, first, nused, moe_w1, moe_w3, moe_w2, l)
        x = _combine(x1.reshape(t, d), mod, rwt.reshape(t, ROUTE_LANES), y2, s).reshape(nb, s, d)
    return x
```

```python
import functools
import math

import jax
import jax.numpy as jnp
from jax import lax
from jax.experimental import pallas as pl
from jax.experimental.pallas import tpu as pltpu

F32 = jnp.float32
BF16 = jnp.bfloat16

NORM_EPS = 1e-6
CHUNK = 64
POOL_WINDOWS = (2, 4, 8, 16)
POOL_HIST = 16
HEAD_DIM = 64
N_HEADS = 4
HEAD_LANES = 2 * HEAD_DIM
ROPE_THETA = 10000.0
SSM_GROUP_DIM = 16
SSM_STATE = 64
MOE_GROUPS = 4
EXPERTS_PER_GROUP = 8
N_EXPERTS = MOE_GROUPS * EXPERTS_PER_GROUP
ROUTE_LANES = 128
NEG_BIG = -1e30
LANES = 128
LOG2E = math.log2(math.e)
SHIFT_FREE_LIMIT = 60.0

VMEM_LIMIT = 48 * 1024 * 1024

SEQ_TILE = 512
ATTN_TILE = 512
SSM_TILE = 256
MOE_TILE = 512


def _cparams(sem):
    return pltpu.CompilerParams(dimension_semantics=sem, vmem_limit_bytes=VMEM_LIMIT)


def _rms(x, g):
    return x * lax.rsqrt(jnp.mean(x * x, axis=-1, keepdims=True) + NORM_EPS) * g


def _dot(a, b):
    return jnp.dot(a, b, preferred_element_type=F32)


def _ada_kernel(c_ref, w_ref, b_ref, o_ref):
    c = c_ref[...]
    act = (c * jax.nn.sigmoid(c)).astype(BF16)
    o_ref[0] = _dot(act, w_ref[0].astype(BF16)) + b_ref[0]


def _ada_mod(c, ada_w, ada_b):
    depth, d, n = ada_w.shape
    nb = c.shape[0]
    tn = 1536
    return pl.pallas_call(
        _ada_kernel,
        grid=(depth, n // tn),
        in_specs=[
            pl.BlockSpec((nb, d), lambda l, j: (0, 0)),
            pl.BlockSpec((1, d, tn), lambda l, j: (l, 0, j)),
            pl.BlockSpec((1, 1, tn), lambda l, j: (l, 0, j)),
        ],
        out_specs=pl.BlockSpec((1, nb, tn), lambda l, j: (l, 0, j)),
        out_shape=jax.ShapeDtypeStruct((depth, nb, n), F32),
        compiler_params=_cparams(("arbitrary", "arbitrary")),
        name="ada_mod",
    )(c, ada_w, ada_b.reshape(depth, 1, n))


def _qk_prep(t, gain, ones, cosf, sinf, scale):
    t2 = t * t
    hi = t2.astype(BF16)
    lo = (t2 - hi.astype(F32)).astype(BF16)
    lane = lax.broadcasted_iota(jnp.int32, (t.shape[0], HEAD_LANES), 1)
    first_half = (lane % HEAD_DIM) < (HEAD_DIM // 2)
    outs = []
    for half in range(2):
        sl = slice(half * 256, half * 256 + 256)
        ss = _dot(hi[:, sl], ones) + _dot(lo[:, sl], ones)
        tn = t[:, sl] * lax.rsqrt(ss * (1.0 / HEAD_DIM) + NORM_EPS)
        for j in range(2):
            th = tn[:, j * HEAD_LANES:(j + 1) * HEAD_LANES] * gain
            rot = jnp.where(first_half, pltpu.roll(th, HEAD_LANES - HEAD_DIM // 2, 1),
                            pltpu.roll(th, HEAD_DIM // 2, 1))
            outs.append(((th * cosf + rot * sinf) * scale).astype(BF16))
    return jnp.concatenate(outs, axis=1)


def _gathered_moe(i, nsteps, d0c, d1c, d0n, d1n, y_hbm, rbuf, sem, w, ts, npc):
    rows = ts * npc
    slot = i % 2
    other = 1 - slot

    def start(d0, d1, s):
        for r in range(ts):
            for k, dref in ((0, d0), (1, d1)):
                pltpu.make_async_copy(y_hbm.at[pl.ds(pl.multiple_of(dref[r], npc), npc), :],
                                      rbuf.at[pl.ds(pl.multiple_of((s * 2 + k) * rows, rows) + r * npc, npc), :],
                                      sem.at[s]).start(priority=k)

    def wait(s):
        pltpu.make_async_copy(y_hbm.at[pl.ds(0, 2 * rows), :],
                              rbuf.at[pl.ds(pl.multiple_of(s * 2 * rows, 2 * rows), 2 * rows), :],
                              sem.at[s]).wait()

    @pl.when(i == 0)
    def _():
        start(d0c, d1c, 0)

    wait(slot)
    start(d0n, d1n, other)

    @pl.when(i == nsteps - 1)
    def _():
        wait(other)

    base = pl.multiple_of(slot * 2 * rows, 2 * rows)
    r0 = jnp.concatenate([rbuf[pl.ds(base + j, ts, stride=npc), :] for j in range(npc)], axis=1)
    r1 = jnp.concatenate([rbuf[pl.ds(base + rows + j, ts, stride=npc), :] for j in range(npc)], axis=1)
    return w[:, 0:1] * r0 + w[:, 1:2] * r1


def _inproj_body(x, m, g_ref, w_ref, qg_ref, kg_ref, ones_ref, cos_ref, sin_ref, ua_ref, q_ref, k_ref, v_ref, uc_ref):
    h = _rms(x, g_ref[...]) * (1.0 + m[1:2]) + m[0:1]
    z = _dot(h.astype(BF16), w_ref[...])
    ua_ref[0] = z[:, 0:256]
    uc_ref[...] = z[:, 1792:2048]
    v_ref[0] = z[:, 1280:1792].astype(BF16)
    cosf = cos_ref[0]
    sinf = sin_ref[0]
    ones = ones_ref[...]
    q_ref[0] = _qk_prep(z[:, 256:768], qg_ref[...], ones, cosf, sinf, HEAD_DIM ** -0.5 * LOG2E)
    k_ref[0] = _qk_prep(z[:, 768:1280], kg_ref[...], ones, cosf, sinf, 1.0)


def _inproj_kernel(x_ref, mod_ref, *rest):
    _inproj_body(x_ref[0], mod_ref[0], *rest)


def _inproj_moe_kernel(d0c, d1c, d0n, d1n, x1_ref, modp_ref, wts_ref, y_hbm, mod_ref, *rest, ts, npc, nsteps):
    *body_refs, x_ref, ua_ref, q_ref, k_ref, v_ref, uc_ref, rbuf, sem = rest
    moe = _gathered_moe(pl.program_id(0), nsteps, d0c, d1c, d0n, d1n, y_hbm, rbuf, sem, wts_ref[0], ts, npc)
    x = x1_ref[0] + modp_ref[0][5:6] * moe
    x_ref[0] = x
    _inproj_body(x, mod_ref[0], *body_refs, ua_ref, q_ref, k_ref, v_ref, uc_ref)


def _moe_combine_kernel(d0c, d1c, d0n, d1n, x1_ref, modp_ref, wts_ref, y_hbm, x_ref, rbuf, sem, *, ts, npc, nsteps):
    moe = _gathered_moe(pl.program_id(0), nsteps, d0c, d1c, d0n, d1n, y_hbm, rbuf, sem, wts_ref[0], ts, npc)
    x_ref[0] = x1_ref[0] + modp_ref[0][5:6] * moe


def _moe_in_specs(nts, ts, d, nsteps):
    nxt = lambda i: (jnp.minimum(i + 1, nsteps - 1),)
    smem = lambda f: pl.BlockSpec((ts,), f, memory_space=pltpu.SMEM)
    return [smem(lambda i: (i,)), smem(lambda i: (i,)), smem(nxt), smem(nxt),
            pl.BlockSpec((1, ts, d), lambda i: (i // nts, i % nts, 0)),
            pl.BlockSpec((1, 6, d), lambda i: (i // nts, 0, 0)),
            pl.BlockSpec((1, ts, ROUTE_LANES), lambda i: (i // nts, i % nts, 0)),
            pl.BlockSpec(memory_space=pl.ANY)]


def _moe_scratch(ts, npc):
    return [pltpu.VMEM((4 * ts * npc, LANES), F32), pltpu.SemaphoreType.DMA((2,))]


def _moe_combine(moe, nb, s, d):
    ts = min(SEQ_TILE, s)
    nts = s // ts
    nsteps = nb * nts
    npc = d // LANES
    d0, d1, x1, modp, wts, y = moe
    return pl.pallas_call(
        functools.partial(_moe_combine_kernel, ts=ts, npc=npc, nsteps=nsteps),
        grid=(nsteps,),
        in_specs=_moe_in_specs(nts, ts, d, nsteps),
        out_specs=pl.BlockSpec((1, ts, d), lambda i: (i // nts, i % nts, 0)),
        out_shape=jax.ShapeDtypeStruct((nb, s, d), F32),
        scratch_shapes=_moe_scratch(ts, npc),
        compiler_params=_cparams(("arbitrary",)),
        name="moe_combine",
    )(d0, d1, d0, d1, x1, modp, wts, y)


def _inproj(x, mod, g, w_in, qg, kg, ones, cosf, sinf, moe=None):
    nb = mod.shape[0]
    d = mod.shape[2]
    s = cosf.shape[1]
    ts = min(SEQ_TILE, s)
    nts = s // ts
    nsteps = nb * nts
    npc = d // LANES
    full = lambda shape: pl.BlockSpec(shape, lambda i: (0,) * len(shape))
    tile = lambda w: pl.BlockSpec((1, ts, w), lambda i: (i // nts, i % nts, 0))
    body_specs = [
        pl.BlockSpec((1, 6, d), lambda i: (i // nts, 0, 0)),
        full((1, d)),
        full(w_in.shape),
        full((1, HEAD_LANES)),
        full((1, HEAD_LANES)),
        full((256, 256)),
        tile(HEAD_LANES),
        tile(HEAD_LANES),
    ]
    out_specs = [tile(256), tile(512), tile(512), tile(512),
                 pl.BlockSpec((ts, 256), lambda i: (i % nts, i // nts))]
    out_shape = [
        jax.ShapeDtypeStruct((nb, s, 256), F32),
        jax.ShapeDtypeStruct((nb, s, 512), BF16),
        jax.ShapeDtypeStruct((nb, s, 512), BF16),
        jax.ShapeDtypeStruct((nb, s, 512), BF16),
        jax.ShapeDtypeStruct((s, nb * 256), F32),
    ]
    body_args = (mod, g, w_in, qg, kg, ones, cosf, sinf)
    if moe is None:
        outs = pl.pallas_call(
            _inproj_kernel,
            grid=(nsteps,),
            in_specs=[tile(d)] + body_specs,
            out_specs=out_specs,
            out_shape=out_shape,
            compiler_params=_cparams(("parallel",)),
            name="inproj",
        )(x, *body_args)
        return (x, *outs)
    d0, d1, x1, modp, wts, y = moe
    return pl.pallas_call(
        functools.partial(_inproj_moe_kernel, ts=ts, npc=npc, nsteps=nsteps),
        grid=(nsteps,),
        in_specs=_moe_in_specs(nts, ts, d, nsteps) + body_specs,
        out_specs=[tile(d)] + out_specs,
        out_shape=[jax.ShapeDtypeStruct((nb, s, d), F32)] + out_shape,
        scratch_shapes=_moe_scratch(ts, npc),
        compiler_params=_cparams(("arbitrary",)),
        name="inproj_moe",
    )(d0, d1, d0, d1, x1, modp, wts, y, *body_args)


def _attn_kernel(par_ref, q_ref, k_ref, v_ref, sg_ref, o_ref, q2_ref, m_ref, l_ref, acc_ref, *, tq, out_scale):
    qi = pl.program_id(2)
    q = q_ref[0]
    lane = lax.broadcasted_iota(jnp.int32, q.shape, 1)
    zero = jnp.zeros_like(q)
    q2_ref[0:tq, :] = jnp.where(lane < HEAD_DIM, q, zero)
    q2_ref[tq:2 * tq, :] = jnp.where(lane >= HEAD_DIM, q, zero)
    m_ref[...] = jnp.full(m_ref.shape, NEG_BIG, F32)
    l_ref[...] = jnp.zeros(l_ref.shape, F32)
    acc_ref[...] = jnp.zeros(acc_ref.shape, F32)

    def scores(ki, masked):
        k = k_ref[0, pl.ds(pl.multiple_of(ki * tq, tq), tq), :]
        s = lax.dot_general(q2_ref[...], k, (((1,), (1,)), ((), ())), preferred_element_type=F32)
        if masked:
            row = lax.broadcasted_iota(jnp.int32, s.shape, 0) % tq
            col = lax.broadcasted_iota(jnp.int32, s.shape, 1)
            s = jnp.where((col // CHUNK) <= (row // CHUNK), s, NEG_BIG)
        return s

    def values(ki):
        return v_ref[0, pl.ds(pl.multiple_of(ki * tq, tq), tq), :]

    def lane_partial(pr):
        tot = pr[:, 0:LANES]
        for j in range(1, pr.shape[1] // LANES):
            tot = tot + pr[:, j * LANES:(j + 1) * LANES]
        return tot

    def bounded_step(ki, masked):
        pr = jnp.exp2(scores(ki, masked))
        l_ref[...] += lane_partial(pr)
        acc_ref[...] += _dot(pr.astype(BF16), values(ki))

    def general_step(ki, masked):
        s = scores(ki, masked)
        m_old = m_ref[...]
        m_new = jnp.maximum(m_old, jnp.max(s, axis=1, keepdims=True))
        alpha = jnp.exp2(m_old - m_new)
        pr = jnp.exp2(s - m_new)
        l_ref[...] = alpha * l_ref[...] + lane_partial(pr)
        acc_ref[...] = alpha * acc_ref[...] + _dot(pr.astype(BF16), values(ki))
        m_ref[...] = m_new

    def sweep(step):
        def body(ki, c):
            step(ki, False)
            return c

        lax.fori_loop(0, qi, body, 0)
        step(qi, True)

    bounded = par_ref[1] > 0.5
    pl.when(bounded)(functools.partial(sweep, bounded_step))
    pl.when(jnp.logical_not(bounded))(functools.partial(sweep, general_step))

    acc = acc_ref[...]
    l = jnp.sum(l_ref[...], axis=1, keepdims=True)
    o = acc[0:tq] / l[0:tq] - par_ref[0] * (acc[tq:2 * tq] / l[tq:2 * tq])
    o_ref[0] = (_rms(o, sg_ref[...]) * out_scale).astype(BF16)


def _attention(q, k, v, par, subln_g, lambda_init):
    nb, s, _ = q.shape
    tq = min(ATTN_TILE, s)
    return pl.pallas_call(
        functools.partial(_attn_kernel, tq=tq, out_scale=1.0 - lambda_init),
        grid=(nb, N_HEADS, s // tq),
        in_specs=[
            pl.BlockSpec(memory_space=pltpu.SMEM),
            pl.BlockSpec((1, tq, HEAD_LANES), lambda b, h, i: (b, i, h)),
            pl.BlockSpec((1, s, HEAD_LANES), lambda b, h, i: (b, 0, h)),
            pl.BlockSpec((1, s, HEAD_LANES), lambda b, h, i: (b, 0, h)),
            pl.BlockSpec((1, HEAD_LANES), lambda b, h, i: (0, 0)),
        ],
        out_specs=pl.BlockSpec((1, tq, HEAD_LANES), lambda b, h, i: (b, i, h)),
        out_shape=jax.ShapeDtypeStruct((nb, s, N_HEADS * HEAD_LANES), BF16),
        scratch_shapes=[
            pltpu.VMEM((2 * tq, HEAD_LANES), BF16),
            pltpu.VMEM((2 * tq, 1), F32),
            pltpu.VMEM((2 * tq, LANES), F32),
            pltpu.VMEM((2 * tq, HEAD_LANES), F32),
        ],
        compiler_params=_cparams(("parallel", "parallel", "arbitrary")),
        name="diff_attn",
    )(par, q, k, v, subln_g)


def _gelu_tanh(x):
    return 0.5 * x * (1.0 + jnp.tanh(math.sqrt(2.0 / math.pi) * (x + 0.044715 * (x * x * x))))


def _ssm_kernel(u_ref, bb_ref, a_ref, ct_ref, dskip_ref, gw_ref, gb_ref, y_ref,
                utm_ref, bur_ref, bui_ref, ytm_ref, xr_ref, xi_ref, *, tt, nb, width, nstate):
    @pl.when(pl.program_id(0) == 0)
    def _():
        xr_ref[...] = jnp.zeros(xr_ref.shape, F32)
        xi_ref[...] = jnp.zeros(xi_ref.shape, F32)

    nlt = width // LANES
    for b in range(nb):
        for j in range(nlt):
            utm_ref[j, pl.ds(b, tt, stride=nb), :] = u_ref[:, b * width + j * LANES:b * width + (j + 1) * LANES]
    u_tm = jnp.concatenate([utm_ref[j] for j in range(nlt)], axis=1)
    bu = _dot(u_tm.astype(BF16), bb_ref[...])
    bur_ref[...] = bu[:, :nstate]
    bui_ref[...] = bu[:, nstate:]

    ar = jnp.broadcast_to(a_ref[0:1, :], (nb, nstate))
    ai = jnp.broadcast_to(a_ref[1:2, :], (nb, nstate))

    def body(t, carry):
        xr, xi = carry
        r0 = pl.multiple_of(t * nb, nb)
        nxr = ar * xr - ai * xi + bur_ref[pl.ds(r0, nb), :]
        nxi = ar * xi + ai * xr + bui_ref[pl.ds(r0, nb), :]
        bur_ref[pl.ds(r0, nb), :] = nxr
        bui_ref[pl.ds(r0, nb), :] = nxi
        return nxr, nxi

    xr, xi = lax.fori_loop(0, tt, body, (xr_ref[...], xi_ref[...]), unroll=8)
    xr_ref[...] = xr
    xi_ref[...] = xi

    y = (_dot(bur_ref[...].astype(BF16), ct_ref[0:nstate, :])
         + _dot(bui_ref[...].astype(BF16), ct_ref[nstate:2 * nstate, :]))
    y = _gelu_tanh(y + dskip_ref[...] * u_tm)
    y = y * jax.nn.sigmoid(_dot(y.astype(BF16), gw_ref[...]) + gb_ref[...])
    for j in range(nlt):
        ytm_ref[j] = y[:, j * LANES:(j + 1) * LANES]
    for b in range(nb):
        for j in range(nlt):
            y_ref[:, b * width + j * LANES:b * width + (j + 1) * LANES] = ytm_ref[j, pl.ds(b, tt, stride=nb), :]


def _ssm(u_tm, bb, a, ct, dskip, gw, gb, nb):
    s = u_tm.shape[0]
    width = u_tm.shape[1] // nb
    nstate = a.shape[1]
    tt = min(SSM_TILE, s)
    full = lambda arr: pl.BlockSpec(arr.shape, lambda c: (0,) * arr.ndim)
    return pl.pallas_call(
        functools.partial(_ssm_kernel, tt=tt, nb=nb, width=width, nstate=nstate),
        grid=(s // tt,),
        in_specs=[pl.BlockSpec((tt, nb * width), lambda c: (c, 0)),
                  full(bb), full(a), full(ct), full(dskip), full(gw), full(gb)],
        out_specs=pl.BlockSpec((tt, nb * width), lambda c: (c, 0)),
        out_shape=jax.ShapeDtypeStruct(u_tm.shape, F32),
        scratch_shapes=[
            pltpu.VMEM((width // LANES, tt * nb, LANES), F32),
            pltpu.VMEM((tt * nb, nstate), F32),
            pltpu.VMEM((tt * nb, nstate), F32),
            pltpu.VMEM((width // LANES, tt * nb, LANES), F32),
            pltpu.VMEM((nb, nstate), F32),
            pltpu.VMEM((nb, nstate), F32),
        ],
        compiler_params=_cparams(("arbitrary",)),
        name="ssm",
    )(u_tm, bb, a, ct, dskip, gw, gb)


def _route(logits):
    lane = lax.broadcasted_iota(jnp.int32, logits.shape, 1)
    far = jnp.int32(4 * ROUTE_LANES)
    gl = jnp.where(lane < MOE_GROUPS, logits, NEG_BIG)
    gmax = jnp.max(gl, axis=1, keepdims=True)
    gidx = jnp.min(jnp.where(gl == gmax, lane, far), axis=1, keepdims=True)
    pg_top = 1.0 / jnp.sum(jnp.exp(gl - gmax), axis=1, keepdims=True)
    in_group = (lane >= MOE_GROUPS) & (((lane - MOE_GROUPS) // EXPERTS_PER_GROUP) == gidx) \
        & (lane < MOE_GROUPS + N_EXPERTS)
    el = jnp.where(in_group, logits, NEG_BIG)
    m1 = jnp.max(el, axis=1, keepdims=True)
    i1 = jnp.min(jnp.where(el == m1, lane, far), axis=1, keepdims=True)
    el2 = jnp.where(lane == i1, NEG_BIG, el)
    m2 = jnp.max(el2, axis=1, keepdims=True)
    i2 = jnp.min(jnp.where(el2 == m2, lane, far), axis=1, keepdims=True)
    e2 = jnp.exp(m2 - m1)
    w1 = pg_top / (1.0 + e2)
    w2 = pg_top * e2 / (1.0 + e2)
    idx = jnp.where(lane == 0, i1 - MOE_GROUPS, jnp.where(lane == 1, i2 - MOE_GROUPS, 0))
    wts = jnp.where(lane == 0, w1, jnp.where(lane == 1, w2, 0.0))
    return idx, wts


def _merge_kernel(x_ref, mod_ref, ua_ref, o_ref, yc_ref, g1_ref, gw_ref, gb_ref, pw_ref, ps_ref,
                  pa_ref, pb_ref, pc_ref, wo_ref, g2_ref, rw_ref, rb_ref,
                  x1_ref, h2_ref, ri_ref, rwt_ref, ubuf_ref, *, ts, d):
    ti = pl.program_id(1)
    x = x_ref[0]
    m = mod_ref[0]
    h = _rms(x, g1_ref[...]) * (1.0 + m[1:2]) + m[0:1]
    gates = jax.nn.sigmoid(_dot(h.astype(BF16), gw_ref[...]) + gb_ref[...])

    @pl.when(ti == 0)
    def _():
        ubuf_ref[0:POOL_HIST, :] = jnp.zeros((POOL_HIST, ubuf_ref.shape[1]), F32)

    u = ua_ref[0]
    ubuf_ref[POOL_HIST:POOL_HIST + ts, :] = u
    t_abs = (ti * ts + lax.broadcasted_iota(jnp.int32, (ts, 1), 0) + 1).astype(F32)
    lane = lax.broadcasted_iota(jnp.int32, u.shape, 1)
    grp = lane // (u.shape[1] // len(POOL_WINDOWS))
    run = jnp.zeros_like(u)
    pooled = jnp.zeros_like(u)
    for j in range(POOL_HIST):
        run = run + ubuf_ref[pl.ds(POOL_HIST - j, ts), :]
        if (j + 1) in POOL_WINDOWS:
            g = POOL_WINDOWS.index(j + 1)
            val = run / jnp.minimum(t_abs, float(j + 1)) - u
            pooled = jnp.where(grp == g, val, pooled)
    ubuf_ref[0:POOL_HIST, :] = ubuf_ref[ts:ts + POOL_HIST, :]
    ya = _dot(pooled.astype(BF16), pw_ref[...]) * ps_ref[...]

    mixed = (gates[:, 0:d] * _dot(ya.astype(BF16), pa_ref[...])
             + gates[:, d:2 * d] * _dot(o_ref[0], pb_ref[...])
             + gates[:, 2 * d:3 * d] * _dot(yc_ref[...].astype(BF16), pc_ref[...]))
    x1 = x + m[2:3] * _dot(mixed.astype(BF16), wo_ref[...])
    x1_ref[0] = x1
    h2 = _rms(x1, g2_ref[...]) * (1.0 + m[4:5]) + m[3:4]
    for j in range(d // LANES):
        h2_ref[pl.ds(j, ts, stride=d // LANES), :] = h2[:, j * LANES:(j + 1) * LANES]
    logits = _dot(h2.astype(BF16), rw_ref[...]) + rb_ref[...]
    idx, wts = _route(logits)
    ri_ref[0] = idx
    rwt_ref[0] = wts


def _merge(x, mod, ua, o, yc_tm, g1, gw, gb, pw, ps, pa, pb, pc, wo, g2, rw, rb):
    nb, s, d = x.shape
    ts = min(SEQ_TILE, s)
    full = lambda arr: pl.BlockSpec(arr.shape, lambda b, t: (0,) * arr.ndim)
    tile = lambda w: pl.BlockSpec((1, ts, w), lambda b, t: (b, t, 0))
    return pl.pallas_call(
        functools.partial(_merge_kernel, ts=ts, d=d),
        grid=(nb, s // ts),
        in_specs=[tile(d), pl.BlockSpec((1, 6, d), lambda b, t: (b, 0, 0)), tile(256), tile(512),
                  pl.BlockSpec((ts, 256), lambda b, t: (t, b)),
                  full(g1), full(gw), full(gb), full(pw), full(ps), full(pa), full(pb), full(pc),
                  full(wo), full(g2), full(rw), full(rb)],
        out_specs=[tile(d), pl.BlockSpec((ts * (d // LANES), LANES), lambda b, t: (b * (s // ts) + t, 0)),
                   tile(ROUTE_LANES), tile(ROUTE_LANES)],
        out_shape=[
            jax.ShapeDtypeStruct((nb, s, d), F32),
            jax.ShapeDtypeStruct((nb * s * (d // LANES), LANES), F32),
            jax.ShapeDtypeStruct((nb, s, ROUTE_LANES), jnp.int32),
            jax.ShapeDtypeStruct((nb, s, ROUTE_LANES), F32),
        ],
        scratch_shapes=[pltpu.VMEM((POOL_HIST + ts, 256), F32)],
        compiler_params=_cparams(("arbitrary", "arbitrary")),
        name="merge_route",
    )(x, mod, ua, o, yc_tm, g1, gw, gb, pw, ps, pa, pb, pc, wo, g2, rw, rb)


def _expert_kernel(blk_e, first, nused, tok_ref, tokn_ref, h_hbm, w1_ref, w3_ref, w2_ref, y_ref,
                   xbuf, w1b, w3b, w2b, gsem, *, tmb, npc):
    i = pl.program_id(0)
    n = nused[0]
    slot = i % 2
    other = 1 - slot
    rows = tmb * npc

    def base(s):
        return pl.multiple_of(s * rows, rows)

    def gather_start(idx_ref, s):
        for r in range(tmb):
            pltpu.make_async_copy(h_hbm.at[pl.ds(pl.multiple_of(idx_ref[r], npc), npc), :],
                                  xbuf.at[pl.ds(base(s) + r * npc, npc), :], gsem.at[s]).start(priority=r % 2)

    def gather_wait(s):
        pltpu.make_async_copy(h_hbm.at[pl.ds(0, rows), :], xbuf.at[pl.ds(base(s), rows), :], gsem.at[s]).wait()

    @pl.when(i == 0)
    def _():
        gather_start(tok_ref, 0)

    @pl.when((i < n) & (first[i] == 1))
    def _():
        w1b[...] = w1_ref[0, 0].astype(BF16)
        w3b[...] = w3_ref[0, 0].astype(BF16)
        w2b[...] = w2_ref[0, 0].astype(BF16)

    @pl.when(i < n)
    def _():
        gather_wait(slot)
        gather_start(tokn_ref, other)
        xb = jnp.concatenate([xbuf[pl.ds(base(slot) + j, tmb, stride=npc), :] for j in range(npc)],
                             axis=1).astype(BF16)
        a = _dot(xb, w1b[...])
        g = _dot(xb, w3b[...])
        mid = (a * jax.nn.sigmoid(a)) * g
        y = _dot(mid.astype(BF16), w2b[...])
        for j in range(npc):
            y_ref[pl.ds(j, tmb, stride=npc), :] = y[:, j * LANES:(j + 1) * LANES]

    @pl.when(i == n - 1)
    def _():
        gather_wait(other)

    @pl.when(i >= n)
    def _():
        y_ref[...] = jnp.zeros(y_ref.shape, F32)


def _experts(h2, tok, blk_e, first, nused, w1, w3, w2, layer):
    d = w1.shape[2]
    npc = d // LANES
    nslot = tok.shape[0]
    tmb = MOE_TILE
    nblk = nslot // tmb
    ff = w1.shape[3]
    smem_blk = lambda f: pl.BlockSpec((tmb,), f, memory_space=pltpu.SMEM)
    grid_spec = pltpu.PrefetchScalarGridSpec(
        num_scalar_prefetch=3,
        grid=(nblk,),
        in_specs=[
            smem_blk(lambda i, be, fi, nu: (i,)),
            smem_blk(lambda i, be, fi, nu: (jnp.minimum(i + 1, nblk - 1),)),
            pl.BlockSpec(memory_space=pl.ANY),
            pl.BlockSpec((1, 1, d, ff), lambda i, be, fi, nu: (layer, be[i], 0, 0)),
            pl.BlockSpec((1, 1, d, ff), lambda i, be, fi, nu: (layer, be[i], 0, 0)),
            pl.BlockSpec((1, 1, ff, d), lambda i, be, fi, nu: (layer, be[i], 0, 0)),
        ],
        out_specs=pl.BlockSpec((tmb * npc, LANES), lambda i, be, fi, nu: (i, 0)),
        scratch_shapes=[
            pltpu.VMEM((2 * tmb * npc, LANES), F32),
            pltpu.VMEM((d, ff), BF16),
            pltpu.VMEM((d, ff), BF16),
            pltpu.VMEM((ff, d), BF16),
            pltpu.SemaphoreType.DMA((2,)),
        ],
    )
    return pl.pallas_call(
        functools.partial(_expert_kernel, tmb=tmb, npc=npc),
        grid_spec=grid_spec,
        out_shape=jax.ShapeDtypeStruct((nslot * npc, LANES), F32),
        compiler_params=_cparams(("arbitrary",)),
        name="experts",
    )(blk_e, first, nused, tok, tok, h2, w1, w3, w2)


def _dispatch_plan(e_idx, tmb, row_scale):
    t = e_idx.shape[0]
    e_flat = e_idx.reshape(-1)
    onehot = (e_flat[:, None] == jnp.arange(N_EXPERTS, dtype=jnp.int32)[None, :]).astype(jnp.int32)
    csum = jnp.cumsum(onehot, axis=0)
    counts = csum[-1]
    padded = (counts + tmb - 1) // tmb * tmb
    pend = jnp.cumsum(padded)
    pstart = pend - padded
    dest = jnp.sum(onehot * (csum - 1 + pstart[None, :]), axis=1)
    nslot = 2 * t + N_EXPERTS * tmb
    src = jnp.full((nslot,), -1, jnp.int32).at[dest].set(jnp.arange(2 * t, dtype=jnp.int32))
    tok = jnp.maximum(src, 0) // 2
    blk_start = jnp.arange(nslot // tmb, dtype=jnp.int32) * tmb
    blk_e = jnp.minimum(jnp.sum(pend[None, :] <= blk_start[:, None], axis=1), N_EXPERTS - 1).astype(jnp.int32)
    first = jnp.concatenate([jnp.ones((1,), jnp.int32), (blk_e[1:] != blk_e[:-1]).astype(jnp.int32)])
    nused = (pend[-1] // tmb).astype(jnp.int32).reshape(1)
    dest2 = (dest * row_scale).reshape(t, 2)
    return tok * row_scale, dest2[:, 0], dest2[:, 1], blk_e, first, nused


def _block_diag(blocks):
    g, r, c = blocks.shape
    eye = jnp.eye(g, dtype=blocks.dtype)
    return jnp.einsum('grc,gh->grhc', blocks, eye).reshape(g * r, g * c)


def _ssm_params(lam_re, lam_im, log_dt, b_re, b_im, c_re, c_im):
    dt = jnp.exp(log_dt)[:, None]
    mag = jnp.exp(lam_re * dt)
    ar, ai = mag * jnp.cos(lam_im * dt), mag * jnp.sin(lam_im * dt)
    nr, ni = ar - 1.0, ai
    den = lam_re * lam_re + lam_im * lam_im
    kr, ki = (nr * lam_re + ni * lam_im) / den, (ni * lam_re - nr * lam_im) / den
    bbr = kr[..., None] * b_re - ki[..., None] * b_im
    bbi = kr[..., None] * b_im + ki[..., None] * b_re
    bb = jnp.concatenate([_block_diag(jnp.swapaxes(bbr, 1, 2)), _block_diag(jnp.swapaxes(bbi, 1, 2))], axis=1)
    ct = jnp.concatenate([_block_diag(jnp.swapaxes(c_re, 1, 2)), -_block_diag(jnp.swapaxes(c_im, 1, 2))], axis=0)
    a = jnp.stack([ar.reshape(-1), ai.reshape(-1)], axis=0)
    return bb.astype(BF16), a, ct.astype(BF16)


def _rope_tables(positions):
    inv = ROPE_THETA ** (-jnp.arange(0, HEAD_DIM, 2, dtype=F32) / HEAD_DIM)
    ang = positions.astype(F32)[..., None] * inv
    cos, sin = jnp.cos(ang), jnp.sin(ang)
    cosf = jnp.concatenate([cos, cos, cos, cos], axis=-1)
    sinf = jnp.concatenate([-sin, sin, -sin, sin], axis=-1)
    return cosf, sinf


def kernel(x, c, positions, ada_w, ada_b, norm1_g, w_in, pool_w, pool_scale, q_norm_g, k_norm_g, lambda_q1, lambda_k1, lambda_q2, lambda_k2, subln_g, lam_re, lam_im, log_dt, b_re, b_im, c_re, c_im, d_skip, glu_w, glu_b, proj_a, proj_b, proj_c, gate_w, gate_b, w_out, norm2_g, router_g_w, router_g_b, router_e_w, router_e_b, moe_w1, moe_w3, moe_w2):
    nb, s, d = x.shape
    depth = ada_w.shape[0]
    cosf, sinf = _rope_tables(positions)
    mod_all = _ada_mod(c, ada_w, ada_b).reshape(depth, nb, 6, d)
    ones = _block_diag(jnp.ones((256 // HEAD_DIM, HEAD_DIM, HEAD_DIM), F32)).astype(BF16)
    row = lambda v: v.reshape(1, -1)
    tile2 = lambda v: jnp.concatenate([v, v]).reshape(1, -1)
    moe = None
    for l in range(depth):
        lambda_init = 0.8 - 0.6 * math.exp(-0.3 * l)
        mod = mod_all[l]
        x, ua, q, k, v, uc_tm = _inproj(x, mod, row(norm1_g[l]), w_in[l].astype(BF16),
                                        tile2(q_norm_g[l]), tile2(k_norm_g[l]), ones, cosf, sinf, moe)
        lam = (jnp.exp(jnp.sum(lambda_q1[l] * lambda_k1[l])) - jnp.exp(jnp.sum(lambda_q2[l] * lambda_k2[l]))
               + lambda_init).astype(F32)
        score_bound = 1.02 * LOG2E * math.sqrt(HEAD_DIM) * jnp.max(jnp.abs(q_norm_g[l])) * jnp.max(jnp.abs(k_norm_g[l]))
        par = jnp.stack([lam, (score_bound <= SHIFT_FREE_LIMIT).astype(F32)])
        o = _attention(q, k, v, par, row(subln_g[l]), lambda_init)
        bb, a, ct = _ssm_params(lam_re[l], lam_im[l], log_dt[l], b_re[l], b_im[l], c_re[l], c_im[l])
        yc_tm = _ssm(uc_tm, bb, a, ct, row(d_skip[l]), glu_w[l].astype(BF16), row(glu_b[l]), nb)
        rw = jnp.zeros((d, ROUTE_LANES), F32).at[:, 0:MOE_GROUPS].set(router_g_w[l]) \
            .at[:, MOE_GROUPS:MOE_GROUPS + N_EXPERTS].set(router_e_w[l]).astype(BF16)
        rb = jnp.zeros((1, ROUTE_LANES), F32).at[0, 0:MOE_GROUPS].set(router_g_b[l]) \
            .at[0, MOE_GROUPS:MOE_GROUPS + N_EXPERTS].set(router_e_b[l])
        x1, h2, ri, rwt = _merge(x, mod, ua, o, yc_tm, row(norm1_g[l]), gate_w[l].astype(BF16), row(gate_b[l]),
                                 _block_diag(pool_w[l]).astype(BF16), row(pool_scale[l]),
                                 proj_a[l].astype(BF16), proj_b[l].astype(BF16), proj_c[l].astype(BF16),
                                 w_out[l].astype(BF16), row(norm2_g[l]), rw, rb)
        t = nb * s
        tok, d0, d1, blk_e, first, nused = _dispatch_plan(ri.reshape(t, ROUTE_LANES)[:, 0:2], MOE_TILE, d // LANES)
        y = _experts(h2, tok, blk_e, first, nused, moe_w1, moe_w3, moe_w2, l)
        moe = (d0, d1, x1, mod, rwt, y)
    return _moe_combine(moe, nb, s, d)
```

```python
import functools
import math

import jax
import jax.numpy as jnp
from jax import lax
from jax.experimental import pallas as pl
from jax.experimental.pallas import tpu as pltpu

F32 = jnp.float32
BF16 = jnp.bfloat16

NORM_EPS = 1e-6
CHUNK = 64
POOL_WINDOWS = (2, 4, 8, 16)
POOL_HIST = 16
HEAD_DIM = 64
N_HEADS = 4
HEAD_LANES = 2 * HEAD_DIM
ROPE_THETA = 10000.0
SSM_GROUP_DIM = 16
SSM_STATE = 64
MOE_GROUPS = 4
EXPERTS_PER_GROUP = 8
N_EXPERTS = MOE_GROUPS * EXPERTS_PER_GROUP
ROUTE_LANES = 128
NEG_BIG = -1e30
LANES = 128
LOG2E = math.log2(math.e)
SHIFT_FREE_LIMIT = 60.0

VMEM_LIMIT = 48 * 1024 * 1024

SEQ_TILE = 512
ATTN_TILE = 512
SSM_TILE = 256
MOE_TILE = 512


def _cparams(sem):
    return pltpu.CompilerParams(dimension_semantics=sem, vmem_limit_bytes=VMEM_LIMIT)


def _rms(x, g):
    return x * lax.rsqrt(jnp.mean(x * x, axis=-1, keepdims=True) + NORM_EPS) * g


def _dot(a, b):
    return jnp.dot(a, b, preferred_element_type=F32)


def _sigmoid(x):
    return 0.5 * jnp.tanh(0.5 * x) + 0.5


def _ada_kernel(c_ref, w_ref, b_ref, o_ref):
    c = c_ref[...]
    act = (c * jax.nn.sigmoid(c)).astype(BF16)
    o_ref[0] = _dot(act, w_ref[0].astype(BF16)) + b_ref[0]


def _ada_mod(c, ada_w, ada_b):
    depth, d, n = ada_w.shape
    nb = c.shape[0]
    tn = 1536
    return pl.pallas_call(
        _ada_kernel,
        grid=(depth, n // tn),
        in_specs=[
            pl.BlockSpec((nb, d), lambda l, j: (0, 0)),
            pl.BlockSpec((1, d, tn), lambda l, j: (l, 0, j)),
            pl.BlockSpec((1, 1, tn), lambda l, j: (l, 0, j)),
        ],
        out_specs=pl.BlockSpec((1, nb, tn), lambda l, j: (l, 0, j)),
        out_shape=jax.ShapeDtypeStruct((depth, nb, n), F32),
        compiler_params=_cparams(("arbitrary", "arbitrary")),
        name="ada_mod",
    )(c, ada_w, ada_b.reshape(depth, 1, n))


def _qk_prep(t, gain, ones, cosf, sinf, scale):
    t2 = t * t
    hi = t2.astype(BF16)
    lo = (t2 - hi.astype(F32)).astype(BF16)
    lane = lax.broadcasted_iota(jnp.int32, (t.shape[0], HEAD_LANES), 1)
    first_half = (lane % HEAD_DIM) < (HEAD_DIM // 2)
    outs = []
    for half in range(2):
        sl = slice(half * 256, half * 256 + 256)
        ss = _dot(hi[:, sl], ones) + _dot(lo[:, sl], ones)
        tn = t[:, sl] * lax.rsqrt(ss * (1.0 / HEAD_DIM) + NORM_EPS)
        for j in range(2):
            th = tn[:, j * HEAD_LANES:(j + 1) * HEAD_LANES] * gain
            rot = jnp.where(first_half, pltpu.roll(th, HEAD_LANES - HEAD_DIM // 2, 1),
                            pltpu.roll(th, HEAD_DIM // 2, 1))
            outs.append(((th * cosf + rot * sinf) * scale).astype(BF16))
    return jnp.concatenate(outs, axis=1)


def _gathered_moe(i, nsteps, d0c, d1c, d0n, d1n, y_hbm, rbuf, sem, w, ts):
    slot = i % 2
    other = 1 - slot

    def start(d0, d1, s):
        for r in range(ts):
            for k, dref in ((0, d0), (1, d1)):
                pltpu.make_async_copy(y_hbm.at[pl.ds(dref[r], 1), :],
                                      rbuf.at[pl.ds(pl.multiple_of((s * 2 + k) * ts, ts) + r, 1), :],
                                      sem.at[s]).start(priority=k)

    def wait(s):
        pltpu.make_async_copy(y_hbm.at[pl.ds(0, 2 * ts), :],
                              rbuf.at[pl.ds(pl.multiple_of(s * 2 * ts, 2 * ts), 2 * ts), :],
                              sem.at[s]).wait()

    @pl.when(i == 0)
    def _():
        start(d0c, d1c, 0)

    wait(slot)
    start(d0n, d1n, other)

    @pl.when(i == nsteps - 1)
    def _():
        wait(other)

    base = pl.multiple_of(slot * 2 * ts, 2 * ts)
    return w[:, 0:1] * rbuf[pl.ds(base, ts), :] + w[:, 1:2] * rbuf[pl.ds(base + ts, ts), :]


def _inproj_body(x, m, g_ref, w_ref, qg_ref, kg_ref, ones_ref, cos_ref, sin_ref, ua_ref, q_ref, k_ref, v_ref, uc_ref):
    h = _rms(x, g_ref[...]) * (1.0 + m[1:2]) + m[0:1]
    z = _dot(h.astype(BF16), w_ref[...])
    ua_ref[0] = z[:, 0:256]
    uc_ref[...] = z[:, 1792:2048]
    v_ref[0] = z[:, 1280:1792].astype(BF16)
    cosf = cos_ref[0]
    sinf = sin_ref[0]
    ones = ones_ref[...]
    q_ref[0] = _qk_prep(z[:, 256:768], qg_ref[...], ones, cosf, sinf, HEAD_DIM ** -0.5 * LOG2E)
    k_ref[0] = _qk_prep(z[:, 768:1280], kg_ref[...], ones, cosf, sinf, 1.0)


def _inproj_kernel(x_ref, mod_ref, *rest):
    _inproj_body(x_ref[0], mod_ref[0], *rest)


def _inproj_moe_kernel(d0c, d1c, d0n, d1n, x1_ref, modp_ref, wts_ref, y_hbm, mod_ref, *rest, ts, nsteps):
    *body_refs, x_ref, ua_ref, q_ref, k_ref, v_ref, uc_ref, rbuf, sem = rest
    moe = _gathered_moe(pl.program_id(0), nsteps, d0c, d1c, d0n, d1n, y_hbm, rbuf, sem, wts_ref[0], ts)
    x = x1_ref[0] + modp_ref[0][5:6] * moe
    x_ref[0] = x
    _inproj_body(x, mod_ref[0], *body_refs, ua_ref, q_ref, k_ref, v_ref, uc_ref)


def _moe_combine_kernel(d0c, d1c, d0n, d1n, x1_ref, modp_ref, wts_ref, y_hbm, x_ref, rbuf, sem, *, ts, nsteps):
    moe = _gathered_moe(pl.program_id(0), nsteps, d0c, d1c, d0n, d1n, y_hbm, rbuf, sem, wts_ref[0], ts)
    x_ref[0] = x1_ref[0] + modp_ref[0][5:6] * moe


def _moe_in_specs(nts, ts, d, nsteps):
    nxt = lambda i: (jnp.minimum(i + 1, nsteps - 1),)
    smem = lambda f: pl.BlockSpec((ts,), f, memory_space=pltpu.SMEM)
    return [smem(lambda i: (i,)), smem(lambda i: (i,)), smem(nxt), smem(nxt),
            pl.BlockSpec((1, ts, d), lambda i: (i // nts, i % nts, 0)),
            pl.BlockSpec((1, 6, d), lambda i: (i // nts, 0, 0)),
            pl.BlockSpec((1, ts, ROUTE_LANES), lambda i: (i // nts, i % nts, 0)),
            pl.BlockSpec(memory_space=pl.ANY)]


def _moe_scratch(ts, d):
    return [pltpu.VMEM((4 * ts, d), F32), pltpu.SemaphoreType.DMA((2,))]


def _moe_combine(moe, nb, s, d):
    ts = min(SEQ_TILE, s)
    nts = s // ts
    nsteps = nb * nts
    d0, d1, x1, modp, wts, y = moe
    return pl.pallas_call(
        functools.partial(_moe_combine_kernel, ts=ts, nsteps=nsteps),
        grid=(nsteps,),
        in_specs=_moe_in_specs(nts, ts, d, nsteps),
        out_specs=pl.BlockSpec((1, ts, d), lambda i: (i // nts, i % nts, 0)),
        out_shape=jax.ShapeDtypeStruct((nb, s, d), F32),
        scratch_shapes=_moe_scratch(ts, d),
        compiler_params=_cparams(("arbitrary",)),
        name="moe_combine",
    )(d0, d1, d0, d1, x1, modp, wts, y)


def _inproj(x, mod, g, w_in, qg, kg, ones, cosf, sinf, moe=None):
    nb = mod.shape[0]
    d = mod.shape[2]
    s = cosf.shape[1]
    ts = min(SEQ_TILE, s)
    nts = s // ts
    nsteps = nb * nts
    full = lambda shape: pl.BlockSpec(shape, lambda i: (0,) * len(shape))
    tile = lambda w: pl.BlockSpec((1, ts, w), lambda i: (i // nts, i % nts, 0))
    body_specs = [
        pl.BlockSpec((1, 6, d), lambda i: (i // nts, 0, 0)),
        full((1, d)),
        full(w_in.shape),
        full((1, HEAD_LANES)),
        full((1, HEAD_LANES)),
        full((256, 256)),
        tile(HEAD_LANES),
        tile(HEAD_LANES),
    ]
    out_specs = [tile(256), tile(512), tile(512), tile(512),
                 pl.BlockSpec((ts, 256), lambda i: (i % nts, i // nts))]
    out_shape = [
        jax.ShapeDtypeStruct((nb, s, 256), F32),
        jax.ShapeDtypeStruct((nb, s, 512), BF16),
        jax.ShapeDtypeStruct((nb, s, 512), BF16),
        jax.ShapeDtypeStruct((nb, s, 512), BF16),
        jax.ShapeDtypeStruct((s, nb * 256), F32),
    ]
    body_args = (mod, g, w_in, qg, kg, ones, cosf, sinf)
    if moe is None:
        outs = pl.pallas_call(
            _inproj_kernel,
            grid=(nsteps,),
            in_specs=[tile(d)] + body_specs,
            out_specs=out_specs,
            out_shape=out_shape,
            compiler_params=_cparams(("parallel",)),
            name="inproj",
        )(x, *body_args)
        return (x, *outs)
    d0, d1, x1, modp, wts, y = moe
    return pl.pallas_call(
        functools.partial(_inproj_moe_kernel, ts=ts, nsteps=nsteps),
        grid=(nsteps,),
        in_specs=_moe_in_specs(nts, ts, d, nsteps) + body_specs,
        out_specs=[tile(d)] + out_specs,
        out_shape=[jax.ShapeDtypeStruct((nb, s, d), F32)] + out_shape,
        scratch_shapes=_moe_scratch(ts, d),
        compiler_params=_cparams(("arbitrary",)),
        name="inproj_moe",
    )(d0, d1, d0, d1, x1, modp, wts, y, *body_args)


def _attn_kernel(par_ref, q_ref, k_ref, v_ref, sg_ref, o_ref, q2_ref, m_ref, l_ref, acc_ref, *, tq, out_scale):
    qi = pl.program_id(2)
    q = q_ref[0]
    lane = lax.broadcasted_iota(jnp.int32, q.shape, 1)
    zero = jnp.zeros_like(q)
    q2_ref[0:tq, :] = jnp.where(lane < HEAD_DIM, q, zero)
    q2_ref[tq:2 * tq, :] = jnp.where(lane >= HEAD_DIM, q, zero)
    m_ref[...] = jnp.full(m_ref.shape, NEG_BIG, F32)
    l_ref[...] = jnp.zeros(l_ref.shape, F32)
    acc_ref[...] = jnp.zeros(acc_ref.shape, F32)

    def scores(ki, masked):
        k = k_ref[0, pl.ds(pl.multiple_of(ki * tq, tq), tq), :]
        s = lax.dot_general(q2_ref[...], k, (((1,), (1,)), ((), ())), preferred_element_type=F32)
        if masked:
            row = lax.broadcasted_iota(jnp.int32, s.shape, 0) % tq
            col = lax.broadcasted_iota(jnp.int32, s.shape, 1)
            s = jnp.where((col // CHUNK) <= (row // CHUNK), s, NEG_BIG)
        return s

    def values(ki):
        return v_ref[0, pl.ds(pl.multiple_of(ki * tq, tq), tq), :]

    def lane_partial(pr):
        tot = pr[:, 0:LANES]
        for j in range(1, pr.shape[1] // LANES):
            tot = tot + pr[:, j * LANES:(j + 1) * LANES]
        return tot

    def bounded_step(ki, masked):
        pr = jnp.exp2(scores(ki, masked))
        l_ref[...] += lane_partial(pr)
        acc_ref[...] += _dot(pr.astype(BF16), values(ki))

    def general_step(ki, masked):
        s = scores(ki, masked)
        m_old = m_ref[...]
        m_new = jnp.maximum(m_old, jnp.max(s, axis=1, keepdims=True))
        alpha = jnp.exp2(m_old - m_new)
        pr = jnp.exp2(s - m_new)
        l_ref[...] = alpha * l_ref[...] + lane_partial(pr)
        acc_ref[...] = alpha * acc_ref[...] + _dot(pr.astype(BF16), values(ki))
        m_ref[...] = m_new

    def sweep(step):
        def body(ki, c):
            step(ki, False)
            return c

        lax.fori_loop(0, qi, body, 0)
        step(qi, True)

    bounded = par_ref[1] > 0.5
    pl.when(bounded)(functools.partial(sweep, bounded_step))
    pl.when(jnp.logical_not(bounded))(functools.partial(sweep, general_step))

    acc = acc_ref[...]
    l = jnp.sum(l_ref[...], axis=1, keepdims=True)
    o = acc[0:tq] / l[0:tq] - par_ref[0] * (acc[tq:2 * tq] / l[tq:2 * tq])
    o_ref[0] = (_rms(o, sg_ref[...]) * out_scale).astype(BF16)


def _attention(q, k, v, par, subln_g, lambda_init):
    nb, s, _ = q.shape
    tq = min(ATTN_TILE, s)
    return pl.pallas_call(
        functools.partial(_attn_kernel, tq=tq, out_scale=1.0 - lambda_init),
        grid=(nb, N_HEADS, s // tq),
        in_specs=[
            pl.BlockSpec(memory_space=pltpu.SMEM),
            pl.BlockSpec((1, tq, HEAD_LANES), lambda b, h, i: (b, i, h)),
            pl.BlockSpec((1, s, HEAD_LANES), lambda b, h, i: (b, 0, h)),
            pl.BlockSpec((1, s, HEAD_LANES), lambda b, h, i: (b, 0, h)),
            pl.BlockSpec((1, HEAD_LANES), lambda b, h, i: (0, 0)),
        ],
        out_specs=pl.BlockSpec((1, tq, HEAD_LANES), lambda b, h, i: (b, i, h)),
        out_shape=jax.ShapeDtypeStruct((nb, s, N_HEADS * HEAD_LANES), BF16),
        scratch_shapes=[
            pltpu.VMEM((2 * tq, HEAD_LANES), BF16),
            pltpu.VMEM((2 * tq, 1), F32),
            pltpu.VMEM((2 * tq, LANES), F32),
            pltpu.VMEM((2 * tq, HEAD_LANES), F32),
        ],
        compiler_params=_cparams(("parallel", "parallel", "arbitrary")),
        name="diff_attn",
    )(par, q, k, v, subln_g)


def _gelu_tanh(x):
    return 0.5 * x * (1.0 + jnp.tanh(math.sqrt(2.0 / math.pi) * (x + 0.044715 * (x * x * x))))


def _ssm_kernel(u_ref, bb_ref, a_ref, ct_ref, dskip_ref, gw_ref, gb_ref, y_ref,
                utm_ref, bur_ref, bui_ref, ytm_ref, xr_ref, xi_ref, *, tt, nb, width, nstate):
    @pl.when(pl.program_id(0) == 0)
    def _():
        xr_ref[...] = jnp.zeros(xr_ref.shape, F32)
        xi_ref[...] = jnp.zeros(xi_ref.shape, F32)

    nlt = width // LANES
    for b in range(nb):
        for j in range(nlt):
            utm_ref[j, pl.ds(b, tt, stride=nb), :] = u_ref[:, b * width + j * LANES:b * width + (j + 1) * LANES]
    u_tm = jnp.concatenate([utm_ref[j] for j in range(nlt)], axis=1)
    bu = _dot(u_tm.astype(BF16), bb_ref[...])
    bur_ref[...] = bu[:, :nstate]
    bui_ref[...] = bu[:, nstate:]

    ar = jnp.broadcast_to(a_ref[0:1, :], (nb, nstate))
    ai = jnp.broadcast_to(a_ref[1:2, :], (nb, nstate))

    def body(t, carry):
        xr, xi = carry
        r0 = pl.multiple_of(t * nb, nb)
        nxr = ar * xr - ai * xi + bur_ref[pl.ds(r0, nb), :]
        nxi = ar * xi + ai * xr + bui_ref[pl.ds(r0, nb), :]
        bur_ref[pl.ds(r0, nb), :] = nxr
        bui_ref[pl.ds(r0, nb), :] = nxi
        return nxr, nxi

    xr, xi = lax.fori_loop(0, tt, body, (xr_ref[...], xi_ref[...]), unroll=8)
    xr_ref[...] = xr
    xi_ref[...] = xi

    y = (_dot(bur_ref[...].astype(BF16), ct_ref[0:nstate, :])
         + _dot(bui_ref[...].astype(BF16), ct_ref[nstate:2 * nstate, :]))
    y = _gelu_tanh(y + dskip_ref[...] * u_tm)
    y = y * _sigmoid(_dot(y.astype(BF16), gw_ref[...]) + gb_ref[...])
    for j in range(nlt):
        ytm_ref[j] = y[:, j * LANES:(j + 1) * LANES]
    for b in range(nb):
        for j in range(nlt):
            y_ref[:, b * width + j * LANES:b * width + (j + 1) * LANES] = ytm_ref[j, pl.ds(b, tt, stride=nb), :]


def _ssm(u_tm, bb, a, ct, dskip, gw, gb, nb):
    s = u_tm.shape[0]
    width = u_tm.shape[1] // nb
    nstate = a.shape[1]
    tt = min(SSM_TILE, s)
    full = lambda arr: pl.BlockSpec(arr.shape, lambda c: (0,) * arr.ndim)
    return pl.pallas_call(
        functools.partial(_ssm_kernel, tt=tt, nb=nb, width=width, nstate=nstate),
        grid=(s // tt,),
        in_specs=[pl.BlockSpec((tt, nb * width), lambda c: (c, 0)),
                  full(bb), full(a), full(ct), full(dskip), full(gw), full(gb)],
        out_specs=pl.BlockSpec((tt, nb * width), lambda c: (c, 0)),
        out_shape=jax.ShapeDtypeStruct(u_tm.shape, F32),
        scratch_shapes=[
            pltpu.VMEM((width // LANES, tt * nb, LANES), F32),
            pltpu.VMEM((tt * nb, nstate), F32),
            pltpu.VMEM((tt * nb, nstate), F32),
            pltpu.VMEM((width // LANES, tt * nb, LANES), F32),
            pltpu.VMEM((nb, nstate), F32),
            pltpu.VMEM((nb, nstate), F32),
        ],
        compiler_params=_cparams(("arbitrary",)),
        name="ssm",
    )(u_tm, bb, a, ct, dskip, gw, gb)


def _route(logits):
    lane = lax.broadcasted_iota(jnp.int32, logits.shape, 1)
    far = jnp.int32(4 * ROUTE_LANES)
    gl = jnp.where(lane < MOE_GROUPS, logits, NEG_BIG)
    gmax = jnp.max(gl, axis=1, keepdims=True)
    gidx = jnp.min(jnp.where(gl == gmax, lane, far), axis=1, keepdims=True)
    pg_top = 1.0 / jnp.sum(jnp.exp(gl - gmax), axis=1, keepdims=True)
    in_group = (lane >= MOE_GROUPS) & (((lane - MOE_GROUPS) // EXPERTS_PER_GROUP) == gidx) \
        & (lane < MOE_GROUPS + N_EXPERTS)
    el = jnp.where(in_group, logits, NEG_BIG)
    m1 = jnp.max(el, axis=1, keepdims=True)
    i1 = jnp.min(jnp.where(el == m1, lane, far), axis=1, keepdims=True)
    el2 = jnp.where(lane == i1, NEG_BIG, el)
    m2 = jnp.max(el2, axis=1, keepdims=True)
    i2 = jnp.min(jnp.where(el2 == m2, lane, far), axis=1, keepdims=True)
    e2 = jnp.exp(m2 - m1)
    w1 = pg_top / (1.0 + e2)
    w2 = pg_top * e2 / (1.0 + e2)
    idx = jnp.where(lane == 0, i1 - MOE_GROUPS, jnp.where(lane == 1, i2 - MOE_GROUPS, 0))
    wts = jnp.where(lane == 0, w1, jnp.where(lane == 1, w2, 0.0))
    return idx, wts


def _merge_kernel(x_ref, mod_ref, ua_ref, o_ref, yc_ref, g1_ref, gw_ref, gb_ref, pw_ref, ps_ref,
                  pa_ref, pb_ref, pc_ref, wo_ref, g2_ref, rw_ref, rb_ref,
                  x1_ref, h2_ref, ri_ref, rwt_ref, ubuf_ref, *, ts, d):
    ti = pl.program_id(1)
    x = x_ref[0]
    m = mod_ref[0]
    h = _rms(x, g1_ref[...]) * (1.0 + m[1:2]) + m[0:1]
    gates = _sigmoid(_dot(h.astype(BF16), gw_ref[...]) + gb_ref[...])

    @pl.when(ti == 0)
    def _():
        ubuf_ref[0:POOL_HIST, :] = jnp.zeros((POOL_HIST, ubuf_ref.shape[1]), F32)

    u = ua_ref[0]
    ubuf_ref[POOL_HIST:POOL_HIST + ts, :] = u
    t_abs = (ti * ts + lax.broadcasted_iota(jnp.int32, (ts, 1), 0) + 1).astype(F32)
    lane = lax.broadcasted_iota(jnp.int32, u.shape, 1)
    grp = lane // (u.shape[1] // len(POOL_WINDOWS))
    run = jnp.zeros_like(u)
    pooled = jnp.zeros_like(u)
    for j in range(POOL_HIST):
        run = run + ubuf_ref[pl.ds(POOL_HIST - j, ts), :]
        if (j + 1) in POOL_WINDOWS:
            g = POOL_WINDOWS.index(j + 1)
            val = run / jnp.minimum(t_abs, float(j + 1)) - u
            pooled = jnp.where(grp == g, val, pooled)
    ubuf_ref[0:POOL_HIST, :] = ubuf_ref[ts:ts + POOL_HIST, :]
    ya = _dot(pooled.astype(BF16), pw_ref[...]) * ps_ref[...]

    mixed = (gates[:, 0:d] * _dot(ya.astype(BF16), pa_ref[...])
             + gates[:, d:2 * d] * _dot(o_ref[0], pb_ref[...])
             + gates[:, 2 * d:3 * d] * _dot(yc_ref[...].astype(BF16), pc_ref[...]))
    x1 = x + m[2:3] * _dot(mixed.astype(BF16), wo_ref[...])
    x1_ref[0] = x1
    h2 = _rms(x1, g2_ref[...]) * (1.0 + m[4:5]) + m[3:4]
    h2_ref[0] = h2
    logits = _dot(h2.astype(BF16), rw_ref[...]) + rb_ref[...]
    idx, wts = _route(logits)
    ri_ref[0] = idx
    rwt_ref[0] = wts


def _merge(x, mod, ua, o, yc_tm, g1, gw, gb, pw, ps, pa, pb, pc, wo, g2, rw, rb):
    nb, s, d = x.shape
    ts = min(SEQ_TILE, s)
    full = lambda arr: pl.BlockSpec(arr.shape, lambda b, t: (0,) * arr.ndim)
    tile = lambda w: pl.BlockSpec((1, ts, w), lambda b, t: (b, t, 0))
    return pl.pallas_call(
        functools.partial(_merge_kernel, ts=ts, d=d),
        grid=(nb, s // ts),
        in_specs=[tile(d), pl.BlockSpec((1, 6, d), lambda b, t: (b, 0, 0)), tile(256), tile(512),
                  pl.BlockSpec((ts, 256), lambda b, t: (t, b)),
                  full(g1), full(gw), full(gb), full(pw), full(ps), full(pa), full(pb), full(pc),
                  full(wo), full(g2), full(rw), full(rb)],
        out_specs=[tile(d), tile(d), tile(ROUTE_LANES), tile(ROUTE_LANES)],
        out_shape=[
            jax.ShapeDtypeStruct((nb, s, d), F32),
            jax.ShapeDtypeStruct((nb, s, d), F32),
            jax.ShapeDtypeStruct((nb, s, ROUTE_LANES), jnp.int32),
            jax.ShapeDtypeStruct((nb, s, ROUTE_LANES), F32),
        ],
        scratch_shapes=[pltpu.VMEM((POOL_HIST + ts, 256), F32)],
        compiler_params=_cparams(("arbitrary", "arbitrary")),
        name="merge_route",
    )(x, mod, ua, o, yc_tm, g1, gw, gb, pw, ps, pa, pb, pc, wo, g2, rw, rb)


def _expert_kernel(blk_e, first, nused, tok_ref, tokn_ref, h_hbm, w1_ref, w3_ref, w2_ref, y_ref,
                   xbuf, w1b, w3b, w2b, gsem, *, tmb):
    i = pl.program_id(0)
    n = nused[0]
    slot = i % 2
    other = 1 - slot

    def base(s):
        return pl.multiple_of(s * tmb, tmb)

    def gather_start(idx_ref, s):
        for r in range(tmb):
            pltpu.make_async_copy(h_hbm.at[pl.ds(idx_ref[r], 1), :],
                                  xbuf.at[pl.ds(base(s) + r, 1), :], gsem.at[s]).start(priority=r % 2)

    def gather_wait(s):
        pltpu.make_async_copy(h_hbm.at[pl.ds(0, tmb), :], xbuf.at[pl.ds(base(s), tmb), :], gsem.at[s]).wait()

    @pl.when(i == 0)
    def _():
        gather_start(tok_ref, 0)

    @pl.when((i < n) & (first[i] == 1))
    def _():
        w1b[...] = w1_ref[0, 0].astype(BF16)
        w3b[...] = w3_ref[0, 0].astype(BF16)
        w2b[...] = w2_ref[0, 0].astype(BF16)

    @pl.when(i < n)
    def _():
        gather_wait(slot)
        gather_start(tokn_ref, other)
        xb = xbuf[pl.ds(base(slot), tmb), :].astype(BF16)
        a = _dot(xb, w1b[...])
        g = _dot(xb, w3b[...])
        mid = (a * _sigmoid(a)) * g
        y_ref[...] = _dot(mid.astype(BF16), w2b[...])

    @pl.when(i == n - 1)
    def _():
        gather_wait(other)

    @pl.when(i >= n)
    def _():
        y_ref[...] = jnp.zeros(y_ref.shape, F32)


def _experts(h2, tok, blk_e, first, nused, w1, w3, w2, layer):
    d = w1.shape[2]
    nslot = tok.shape[0]
    tmb = MOE_TILE
    nblk = nslot // tmb
    ff = w1.shape[3]
    smem_blk = lambda f: pl.BlockSpec((tmb,), f, memory_space=pltpu.SMEM)
    grid_spec = pltpu.PrefetchScalarGridSpec(
        num_scalar_prefetch=3,
        grid=(nblk,),
        in_specs=[
            smem_blk(lambda i, be, fi, nu: (i,)),
            smem_blk(lambda i, be, fi, nu: (jnp.minimum(i + 1, nblk - 1),)),
            pl.BlockSpec(memory_space=pl.ANY),
            pl.BlockSpec((1, 1, d, ff), lambda i, be, fi, nu: (layer, be[i], 0, 0)),
            pl.BlockSpec((1, 1, d, ff), lambda i, be, fi, nu: (layer, be[i], 0, 0)),
            pl.BlockSpec((1, 1, ff, d), lambda i, be, fi, nu: (layer, be[i], 0, 0)),
        ],
        out_specs=pl.BlockSpec((tmb, d), lambda i, be, fi, nu: (i, 0)),
        scratch_shapes=[
            pltpu.VMEM((2 * tmb, d), F32),
            pltpu.VMEM((d, ff), BF16),
            pltpu.VMEM((d, ff), BF16),
            pltpu.VMEM((ff, d), BF16),
            pltpu.SemaphoreType.DMA((2,)),
        ],
    )
    return pl.pallas_call(
        functools.partial(_expert_kernel, tmb=tmb),
        grid_spec=grid_spec,
        out_shape=jax.ShapeDtypeStruct((nslot, d), F32),
        compiler_params=_cparams(("arbitrary",)),
        name="experts",
    )(blk_e, first, nused, tok, tok, h2, w1, w3, w2)


def _dispatch_plan(e_idx, tmb):
    t = e_idx.shape[0]
    e_flat = e_idx.reshape(-1)
    onehot = (e_flat[:, None] == jnp.arange(N_EXPERTS, dtype=jnp.int32)[None, :]).astype(jnp.int32)
    csum = jnp.cumsum(onehot, axis=0)
    counts = csum[-1]
    padded = (counts + tmb - 1) // tmb * tmb
    pend = jnp.cumsum(padded)
    pstart = pend - padded
    dest = jnp.sum(onehot * (csum - 1 + pstart[None, :]), axis=1)
    nslot = 2 * t + N_EXPERTS * tmb
    src = jnp.full((nslot,), -1, jnp.int32).at[dest].set(jnp.arange(2 * t, dtype=jnp.int32))
    tok = jnp.maximum(src, 0) // 2
    blk_start = jnp.arange(nslot // tmb, dtype=jnp.int32) * tmb
    blk_e = jnp.minimum(jnp.sum(pend[None, :] <= blk_start[:, None], axis=1), N_EXPERTS - 1).astype(jnp.int32)
    first = jnp.concatenate([jnp.ones((1,), jnp.int32), (blk_e[1:] != blk_e[:-1]).astype(jnp.int32)])
    nused = (pend[-1] // tmb).astype(jnp.int32).reshape(1)
    dest2 = dest.reshape(t, 2)
    return tok, dest2[:, 0], dest2[:, 1], blk_e, first, nused


def _block_diag(blocks):
    g, r, c = blocks.shape
    eye = jnp.eye(g, dtype=blocks.dtype)
    return jnp.einsum('grc,gh->grhc', blocks, eye).reshape(g * r, g * c)


def _ssm_params(lam_re, lam_im, log_dt, b_re, b_im, c_re, c_im):
    dt = jnp.exp(log_dt)[:, None]
    mag = jnp.exp(lam_re * dt)
    ar, ai = mag * jnp.cos(lam_im * dt), mag * jnp.sin(lam_im * dt)
    nr, ni = ar - 1.0, ai
    den = lam_re * lam_re + lam_im * lam_im
    kr, ki = (nr * lam_re + ni * lam_im) / den, (ni * lam_re - nr * lam_im) / den
    bbr = kr[..., None] * b_re - ki[..., None] * b_im
    bbi = kr[..., None] * b_im + ki[..., None] * b_re
    bb = jnp.concatenate([_block_diag(jnp.swapaxes(bbr, 1, 2)), _block_diag(jnp.swapaxes(bbi, 1, 2))], axis=1)
    ct = jnp.concatenate([_block_diag(jnp.swapaxes(c_re, 1, 2)), -_block_diag(jnp.swapaxes(c_im, 1, 2))], axis=0)
    a = jnp.stack([ar.reshape(-1), ai.reshape(-1)], axis=0)
    return bb.astype(BF16), a, ct.astype(BF16)


def _rope_tables(positions):
    inv = ROPE_THETA ** (-jnp.arange(0, HEAD_DIM, 2, dtype=F32) / HEAD_DIM)
    ang = positions.astype(F32)[..., None] * inv
    cos, sin = jnp.cos(ang), jnp.sin(ang)
    cosf = jnp.concatenate([cos, cos, cos, cos], axis=-1)
    sinf = jnp.concatenate([-sin, sin, -sin, sin], axis=-1)
    return cosf, sinf


def kernel(x, c, positions, ada_w, ada_b, norm1_g, w_in, pool_w, pool_scale, q_norm_g, k_norm_g, lambda_q1, lambda_k1, lambda_q2, lambda_k2, subln_g, lam_re, lam_im, log_dt, b_re, b_im, c_re, c_im, d_skip, glu_w, glu_b, proj_a, proj_b, proj_c, gate_w, gate_b, w_out, norm2_g, router_g_w, router_g_b, router_e_w, router_e_b, moe_w1, moe_w3, moe_w2):
    nb, s, d = x.shape
    depth = ada_w.shape[0]
    cosf, sinf = _rope_tables(positions)
    mod_all = _ada_mod(c, ada_w, ada_b).reshape(depth, nb, 6, d)
    ones = _block_diag(jnp.ones((256 // HEAD_DIM, HEAD_DIM, HEAD_DIM), F32)).astype(BF16)
    row = lambda v: v.reshape(1, -1)
    tile2 = lambda v: jnp.concatenate([v, v]).reshape(1, -1)
    moe = None
    for l in range(depth):
        lambda_init = 0.8 - 0.6 * math.exp(-0.3 * l)
        mod = mod_all[l]
        x, ua, q, k, v, uc_tm = _inproj(x, mod, row(norm1_g[l]), w_in[l].astype(BF16),
                                        tile2(q_norm_g[l]), tile2(k_norm_g[l]), ones, cosf, sinf, moe)
        lam = (jnp.exp(jnp.sum(lambda_q1[l] * lambda_k1[l])) - jnp.exp(jnp.sum(lambda_q2[l] * lambda_k2[l]))
               + lambda_init).astype(F32)
        score_bound = 1.02 * LOG2E * math.sqrt(HEAD_DIM) * jnp.max(jnp.abs(q_norm_g[l])) * jnp.max(jnp.abs(k_norm_g[l]))
        par = jnp.stack([lam, (score_bound <= SHIFT_FREE_LIMIT).astype(F32)])
        o = _attention(q, k, v, par, row(subln_g[l]), lambda_init)
        bb, a, ct = _ssm_params(lam_re[l], lam_im[l], log_dt[l], b_re[l], b_im[l], c_re[l], c_im[l])
        yc_tm = _ssm(uc_tm, bb, a, ct, row(d_skip[l]), glu_w[l].astype(BF16), row(glu_b[l]), nb)
        rw = jnp.zeros((d, ROUTE_LANES), F32).at[:, 0:MOE_GROUPS].set(router_g_w[l]) \
            .at[:, MOE_GROUPS:MOE_GROUPS + N_EXPERTS].set(router_e_w[l]).astype(BF16)
        rb = jnp.zeros((1, ROUTE_LANES), F32).at[0, 0:MOE_GROUPS].set(router_g_b[l]) \
            .at[0, MOE_GROUPS:MOE_GROUPS + N_EXPERTS].set(router_e_b[l])
        x1, h2, ri, rwt = _merge(x, mod, ua, o, yc_tm, row(norm1_g[l]), gate_w[l].astype(BF16), row(gate_b[l]),
                                 _block_diag(pool_w[l]).astype(BF16), row(pool_scale[l]),
                                 proj_a[l].astype(BF16), proj_b[l].astype(BF16), proj_c[l].astype(BF16),
                                 w_out[l].astype(BF16), row(norm2_g[l]), rw, rb)
        t = nb * s
        tok, d0, d1, blk_e, first, nused = _dispatch_plan(ri.reshape(t, ROUTE_LANES)[:, 0:2], MOE_TILE)
        y = _experts(h2.reshape(t, d), tok, blk_e, first, nused, moe_w1, moe_w3, moe_w2, l)
        moe = (d0, d1, x1, mod, rwt, y)
    return _moe_combine(moe, nb, s, d)
```

```python
import functools
import math

import jax
import jax.numpy as jnp
from jax import lax
from jax.experimental import pallas as pl
from jax.experimental.pallas import tpu as pltpu

F32 = jnp.float32
BF16 = jnp.bfloat16

NORM_EPS = 1e-6
CHUNK = 64
POOL_WINDOWS = (2, 4, 8, 16)
POOL_HIST = 16
HEAD_DIM = 64
N_HEADS = 4
HEAD_LANES = 2 * HEAD_DIM
ROPE_THETA = 10000.0
SSM_GROUP_DIM = 16
SSM_STATE = 64
MOE_GROUPS = 4
EXPERTS_PER_GROUP = 8
N_EXPERTS = MOE_GROUPS * EXPERTS_PER_GROUP
ROUTE_LANES = 128
NEG_BIG = -1e30
LANES = 128
PIECE = 256
LOG2E = math.log2(math.e)
SHIFT_FREE_LIMIT = 60.0

VMEM_LIMIT = 48 * 1024 * 1024

SEQ_TILE = 512
ATTN_TILE = 512
SSM_TILE = 256
MOE_TILE = 512


def _cparams(sem):
    return pltpu.CompilerParams(dimension_semantics=sem, vmem_limit_bytes=VMEM_LIMIT)


def _rms(x, g):
    return x * lax.rsqrt(jnp.mean(x * x, axis=-1, keepdims=True) + NORM_EPS) * g


def _dot(a, b):
    return jnp.dot(a, b, preferred_element_type=F32)


def _sigmoid(x):
    return 0.5 * jnp.tanh(0.5 * x) + 0.5


def _dot_pieces(a, w_ref, between):
    n = w_ref.shape[1] // PIECE
    parts = []
    for p in range(n):
        parts.append(_dot(a, w_ref[:, p * PIECE:(p + 1) * PIECE]))
        between(p, n)
    return jnp.concatenate(parts, axis=1)


def _ada_kernel(c_ref, w_ref, b_ref, o_ref):
    c = c_ref[...]
    act = (c * jax.nn.sigmoid(c)).astype(BF16)
    o_ref[0] = _dot(act, w_ref[0].astype(BF16)) + b_ref[0]


def _ada_mod(c, ada_w, ada_b):
    depth, d, n = ada_w.shape
    nb = c.shape[0]
    tn = 1536
    return pl.pallas_call(
        _ada_kernel,
        grid=(depth, n // tn),
        in_specs=[
            pl.BlockSpec((nb, d), lambda l, j: (0, 0)),
            pl.BlockSpec((1, d, tn), lambda l, j: (l, 0, j)),
            pl.BlockSpec((1, 1, tn), lambda l, j: (l, 0, j)),
        ],
        out_specs=pl.BlockSpec((1, nb, tn), lambda l, j: (l, 0, j)),
        out_shape=jax.ShapeDtypeStruct((depth, nb, n), F32),
        compiler_params=_cparams(("arbitrary", "arbitrary")),
        name="ada_mod",
    )(c, ada_w, ada_b.reshape(depth, 1, n))


def _qk_prep(t, gain, ones, cosf, sinf, scale):
    t2 = t * t
    hi = t2.astype(BF16)
    lo = (t2 - hi.astype(F32)).astype(BF16)
    lane = lax.broadcasted_iota(jnp.int32, (t.shape[0], HEAD_LANES), 1)
    first_half = (lane % HEAD_DIM) < (HEAD_DIM // 2)
    outs = []
    for half in range(2):
        sl = slice(half * 256, half * 256 + 256)
        ss = _dot(hi[:, sl], ones) + _dot(lo[:, sl], ones)
        tn = t[:, sl] * lax.rsqrt(ss * (1.0 / HEAD_DIM) + NORM_EPS)
        for j in range(2):
            th = tn[:, j * HEAD_LANES:(j + 1) * HEAD_LANES] * gain
            rot = jnp.where(first_half, pltpu.roll(th, HEAD_LANES - HEAD_DIM // 2, 1),
                            pltpu.roll(th, HEAD_DIM // 2, 1))
            outs.append(((th * cosf + rot * sinf) * scale).astype(BF16))
    return jnp.concatenate(outs, axis=1)


def _gathered_moe(i, nsteps, d0c, d1c, d0n, d1n, y_hbm, rbuf, sem, w, ts, npc):
    rows = ts * npc
    slot = i % 2
    other = 1 - slot

    def start(d0, d1, s, lo=0, hi=ts):
        for r in range(lo, hi):
            for k, dref in ((0, d0), (1, d1)):
                pltpu.make_async_copy(y_hbm.at[pl.ds(pl.multiple_of(dref[r], npc), npc), :],
                                      rbuf.at[pl.ds(pl.multiple_of((s * 2 + k) * rows, rows) + r * npc, npc), :],
                                      sem.at[s]).start(priority=k)

    def wait(s):
        pltpu.make_async_copy(y_hbm.at[pl.ds(0, 2 * rows), :],
                              rbuf.at[pl.ds(pl.multiple_of(s * 2 * rows, 2 * rows), 2 * rows), :],
                              sem.at[s]).wait()

    @pl.when(i == 0)
    def _():
        start(d0c, d1c, 0)

    wait(slot)
    base = pl.multiple_of(slot * 2 * rows, 2 * rows)
    r0 = jnp.concatenate([rbuf[pl.ds(base + j, ts, stride=npc), :] for j in range(npc)], axis=1)
    r1 = jnp.concatenate([rbuf[pl.ds(base + rows + j, ts, stride=npc), :] for j in range(npc)], axis=1)

    def issue_next(p, n):
        start(d0n, d1n, other, p * ts // n, (p + 1) * ts // n)
        if p == n - 1:
            @pl.when(i == nsteps - 1)
            def _():
                wait(other)

    return w[:, 0:1] * r0 + w[:, 1:2] * r1, issue_next


def _inproj_body(x, m, g_ref, w_ref, qg_ref, kg_ref, ones_ref, cos_ref, sin_ref, ua_ref, q_ref, k_ref, v_ref, uc_ref,
                 between=lambda p, n: None):
    h = _rms(x, g_ref[...]) * (1.0 + m[1:2]) + m[0:1]
    z = _dot_pieces(h.astype(BF16), w_ref, between)
    ua_ref[0] = z[:, 0:256]
    uc_ref[...] = z[:, 1792:2048]
    v_ref[0] = z[:, 1280:1792].astype(BF16)
    cosf = cos_ref[0]
    sinf = sin_ref[0]
    ones = ones_ref[...]
    q_ref[0] = _qk_prep(z[:, 256:768], qg_ref[...], ones, cosf, sinf, HEAD_DIM ** -0.5 * LOG2E)
    k_ref[0] = _qk_prep(z[:, 768:1280], kg_ref[...], ones, cosf, sinf, 1.0)


def _inproj_kernel(x_ref, mod_ref, *rest):
    _inproj_body(x_ref[0], mod_ref[0], *rest)


def _inproj_moe_kernel(d0c, d1c, d0n, d1n, x1_ref, modp_ref, wts_ref, y_hbm, mod_ref, *rest, ts, npc, nsteps):
    *body_refs, x_ref, ua_ref, q_ref, k_ref, v_ref, uc_ref, rbuf, sem = rest
    moe, issue_next = _gathered_moe(pl.program_id(0), nsteps, d0c, d1c, d0n, d1n, y_hbm, rbuf, sem, wts_ref[0],
                                    ts, npc)
    x = x1_ref[0] + modp_ref[0][5:6] * moe
    x_ref[0] = x
    _inproj_body(x, mod_ref[0], *body_refs, ua_ref, q_ref, k_ref, v_ref, uc_ref, between=issue_next)


def _moe_combine_kernel(d0c, d1c, d0n, d1n, x1_ref, modp_ref, wts_ref, y_hbm, x_ref, rbuf, sem, *, ts, npc, nsteps):
    moe, issue_next = _gathered_moe(pl.program_id(0), nsteps, d0c, d1c, d0n, d1n, y_hbm, rbuf, sem, wts_ref[0],
                                    ts, npc)
    issue_next(0, 1)
    x_ref[0] = x1_ref[0] + modp_ref[0][5:6] * moe


def _moe_in_specs(nts, ts, d, nsteps):
    nxt = lambda i: (jnp.minimum(i + 1, nsteps - 1),)
    smem = lambda f: pl.BlockSpec((ts,), f, memory_space=pltpu.SMEM)
    return [smem(lambda i: (i,)), smem(lambda i: (i,)), smem(nxt), smem(nxt),
            pl.BlockSpec((1, ts, d), lambda i: (i // nts, i % nts, 0)),
            pl.BlockSpec((1, 6, d), lambda i: (i // nts, 0, 0)),
            pl.BlockSpec((1, ts, ROUTE_LANES), lambda i: (i // nts, i % nts, 0)),
            pl.BlockSpec(memory_space=pl.ANY)]


def _moe_scratch(ts, npc):
    return [pltpu.VMEM((4 * ts * npc, LANES), F32), pltpu.SemaphoreType.DMA((2,))]


def _moe_combine(moe, nb, s, d):
    ts = min(SEQ_TILE, s)
    nts = s // ts
    nsteps = nb * nts
    npc = d // LANES
    d0, d1, x1, modp, wts, y = moe
    return pl.pallas_call(
        functools.partial(_moe_combine_kernel, ts=ts, npc=npc, nsteps=nsteps),
        grid=(nsteps,),
        in_specs=_moe_in_specs(nts, ts, d, nsteps),
        out_specs=pl.BlockSpec((1, ts, d), lambda i: (i // nts, i % nts, 0)),
        out_shape=jax.ShapeDtypeStruct((nb, s, d), F32),
        scratch_shapes=_moe_scratch(ts, npc),
        compiler_params=_cparams(("arbitrary",)),
        name="moe_combine",
    )(d0, d1, d0, d1, x1, modp, wts, y)


def _inproj(x, mod, g, w_in, qg, kg, ones, cosf, sinf, moe=None):
    nb = mod.shape[0]
    d = mod.shape[2]
    s = cosf.shape[1]
    ts = min(SEQ_TILE, s)
    nts = s // ts
    nsteps = nb * nts
    npc = d // LANES
    full = lambda shape: pl.BlockSpec(shape, lambda i: (0,) * len(shape))
    tile = lambda w: pl.BlockSpec((1, ts, w), lambda i: (i // nts, i % nts, 0))
    body_specs = [
        pl.BlockSpec((1, 6, d), lambda i: (i // nts, 0, 0)),
        full((1, d)),
        full(w_in.shape),
        full((1, HEAD_LANES)),
        full((1, HEAD_LANES)),
        full((256, 256)),
        tile(HEAD_LANES),
        tile(HEAD_LANES),
    ]
    out_specs = [tile(256), tile(512), tile(512), tile(512),
                 pl.BlockSpec((ts, 256), lambda i: (i % nts, i // nts))]
    out_shape = [
        jax.ShapeDtypeStruct((nb, s, 256), F32),
        jax.ShapeDtypeStruct((nb, s, 512), BF16),
        jax.ShapeDtypeStruct((nb, s, 512), BF16),
        jax.ShapeDtypeStruct((nb, s, 512), BF16),
        jax.ShapeDtypeStruct((s, nb * 256), F32),
    ]
    body_args = (mod, g, w_in, qg, kg, ones, cosf, sinf)
    if moe is None:
        outs = pl.pallas_call(
            _inproj_kernel,
            grid=(nsteps,),
            in_specs=[tile(d)] + body_specs,
            out_specs=out_specs,
            out_shape=out_shape,
            compiler_params=_cparams(("parallel",)),
            name="inproj",
        )(x, *body_args)
        return (x, *outs)
    d0, d1, x1, modp, wts, y = moe
    return pl.pallas_call(
        functools.partial(_inproj_moe_kernel, ts=ts, npc=npc, nsteps=nsteps),
        grid=(nsteps,),
        in_specs=_moe_in_specs(nts, ts, d, nsteps) + body_specs,
        out_specs=[tile(d)] + out_specs,
        out_shape=[jax.ShapeDtypeStruct((nb, s, d), F32)] + out_shape,
        scratch_shapes=_moe_scratch(ts, npc),
        compiler_params=_cparams(("arbitrary",)),
        name="inproj_moe",
    )(d0, d1, d0, d1, x1, modp, wts, y, *body_args)


def _attn_kernel(par_ref, q_ref, k_ref, v_ref, sg_ref, o_ref, q2_ref, m_ref, l_ref, acc_ref, *, tq, out_scale):
    qi = pl.program_id(2)
    q = q_ref[0]
    lane = lax.broadcasted_iota(jnp.int32, q.shape, 1)
    zero = jnp.zeros_like(q)
    q2_ref[0:tq, :] = jnp.where(lane < HEAD_DIM, q, zero)
    q2_ref[tq:2 * tq, :] = jnp.where(lane >= HEAD_DIM, q, zero)
    m_ref[...] = jnp.full(m_ref.shape, NEG_BIG, F32)
    l_ref[...] = jnp.zeros(l_ref.shape, F32)
    acc_ref[...] = jnp.zeros(acc_ref.shape, F32)

    def scores(ki, masked):
        k = k_ref[0, pl.ds(pl.multiple_of(ki * tq, tq), tq), :]
        s = lax.dot_general(q2_ref[...], k, (((1,), (1,)), ((), ())), preferred_element_type=F32)
        if masked:
            row = lax.broadcasted_iota(jnp.int32, s.shape, 0) % tq
            col = lax.broadcasted_iota(jnp.int32, s.shape, 1)
            s = jnp.where((col // CHUNK) <= (row // CHUNK), s, NEG_BIG)
        return s

    def values(ki):
        return v_ref[0, pl.ds(pl.multiple_of(ki * tq, tq), tq), :]

    def lane_partial(pr):
        tot = pr[:, 0:LANES]
        for j in range(1, pr.shape[1] // LANES):
            tot = tot + pr[:, j * LANES:(j + 1) * LANES]
        return tot

    def bounded_step(ki, masked):
        pr = jnp.exp2(scores(ki, masked))
        l_ref[...] += lane_partial(pr)
        acc_ref[...] += _dot(pr.astype(BF16), values(ki))

    def general_step(ki, masked):
        s = scores(ki, masked)
        m_old = m_ref[...]
        m_new = jnp.maximum(m_old, jnp.max(s, axis=1, keepdims=True))
        alpha = jnp.exp2(m_old - m_new)
        pr = jnp.exp2(s - m_new)
        l_ref[...] = alpha * l_ref[...] + lane_partial(pr)
        acc_ref[...] = alpha * acc_ref[...] + _dot(pr.astype(BF16), values(ki))
        m_ref[...] = m_new

    def sweep(step):
        def body(ki, c):
            step(ki, False)
            return c

        lax.fori_loop(0, qi, body, 0)
        step(qi, True)

    bounded = par_ref[1] > 0.5
    pl.when(bounded)(functools.partial(sweep, bounded_step))
    pl.when(jnp.logical_not(bounded))(functools.partial(sweep, general_step))

    acc = acc_ref[...]
    l = jnp.sum(l_ref[...], axis=1, keepdims=True)
    o = acc[0:tq] / l[0:tq] - par_ref[0] * (acc[tq:2 * tq] / l[tq:2 * tq])
    o_ref[0] = (_rms(o, sg_ref[...]) * out_scale).astype(BF16)


def _attention(q, k, v, par, subln_g, lambda_init):
    nb, s, _ = q.shape
    tq = min(ATTN_TILE, s)
    return pl.pallas_call(
        functools.partial(_attn_kernel, tq=tq, out_scale=1.0 - lambda_init),
        grid=(nb, N_HEADS, s // tq),
        in_specs=[
            pl.BlockSpec(memory_space=pltpu.SMEM),
            pl.BlockSpec((1, tq, HEAD_LANES), lambda b, h, i: (b, i, h)),
            pl.BlockSpec((1, s, HEAD_LANES), lambda b, h, i: (b, 0, h)),
            pl.BlockSpec((1, s, HEAD_LANES), lambda b, h, i: (b, 0, h)),
            pl.BlockSpec((1, HEAD_LANES), lambda b, h, i: (0, 0)),
        ],
        out_specs=pl.BlockSpec((1, tq, HEAD_LANES), lambda b, h, i: (b, i, h)),
        out_shape=jax.ShapeDtypeStruct((nb, s, N_HEADS * HEAD_LANES), BF16),
        scratch_shapes=[
            pltpu.VMEM((2 * tq, HEAD_LANES), BF16),
            pltpu.VMEM((2 * tq, 1), F32),
            pltpu.VMEM((2 * tq, LANES), F32),
            pltpu.VMEM((2 * tq, HEAD_LANES), F32),
        ],
        compiler_params=_cparams(("parallel", "parallel", "arbitrary")),
        name="diff_attn",
    )(par, q, k, v, subln_g)


def _gelu_tanh(x):
    return 0.5 * x * (1.0 + jnp.tanh(math.sqrt(2.0 / math.pi) * (x + 0.044715 * (x * x * x))))


def _ssm_kernel(u_ref, bb_ref, a_ref, ct_ref, dskip_ref, gw_ref, gb_ref, y_ref,
                utm_ref, bur_ref, bui_ref, ytm_ref, xr_ref, xi_ref, *, tt, nb, width, nstate):
    @pl.when(pl.program_id(0) == 0)
    def _():
        xr_ref[...] = jnp.zeros(xr_ref.shape, F32)
        xi_ref[...] = jnp.zeros(xi_ref.shape, F32)

    nlt = width // LANES
    for b in range(nb):
        for j in range(nlt):
            utm_ref[j, pl.ds(b, tt, stride=nb), :] = u_ref[:, b * width + j * LANES:b * width + (j + 1) * LANES]
    u_tm = jnp.concatenate([utm_ref[j] for j in range(nlt)], axis=1)
    bu = _dot(u_tm.astype(BF16), bb_ref[...])
    bur_ref[...] = bu[:, :nstate]
    bui_ref[...] = bu[:, nstate:]

    ar = jnp.broadcast_to(a_ref[0:1, :], (nb, nstate))
    ai = jnp.broadcast_to(a_ref[1:2, :], (nb, nstate))

    def body(t, carry):
        xr, xi = carry
        r0 = pl.multiple_of(t * nb, nb)
        nxr = ar * xr - ai * xi + bur_ref[pl.ds(r0, nb), :]
        nxi = ar * xi + ai * xr + bui_ref[pl.ds(r0, nb), :]
        bur_ref[pl.ds(r0, nb), :] = nxr
        bui_ref[pl.ds(r0, nb), :] = nxi
        return nxr, nxi

    xr, xi = lax.fori_loop(0, tt, body, (xr_ref[...], xi_ref[...]), unroll=8)
    xr_ref[...] = xr
    xi_ref[...] = xi

    y = (_dot(bur_ref[...].astype(BF16), ct_ref[0:nstate, :])
         + _dot(bui_ref[...].astype(BF16), ct_ref[nstate:2 * nstate, :]))
    y = _gelu_tanh(y + dskip_ref[...] * u_tm)
    y = y * _sigmoid(_dot(y.astype(BF16), gw_ref[...]) + gb_ref[...])
    for j in range(nlt):
        ytm_ref[j] = y[:, j * LANES:(j + 1) * LANES]
    for b in range(nb):
        for j in range(nlt):
            y_ref[:, b * width + j * LANES:b * width + (j + 1) * LANES] = ytm_ref[j, pl.ds(b, tt, stride=nb), :]


def _ssm(u_tm, bb, a, ct, dskip, gw, gb, nb):
    s = u_tm.shape[0]
    width = u_tm.shape[1] // nb
    nstate = a.shape[1]
    tt = min(SSM_TILE, s)
    full = lambda arr: pl.BlockSpec(arr.shape, lambda c: (0,) * arr.ndim)
    return pl.pallas_call(
        functools.partial(_ssm_kernel, tt=tt, nb=nb, width=width, nstate=nstate),
        grid=(s // tt,),
        in_specs=[pl.BlockSpec((tt, nb * width), lambda c: (c, 0)),
                  full(bb), full(a), full(ct), full(dskip), full(gw), full(gb)],
        out_specs=pl.BlockSpec((tt, nb * width), lambda c: (c, 0)),
        out_shape=jax.ShapeDtypeStruct(u_tm.shape, F32),
        scratch_shapes=[
            pltpu.VMEM((width // LANES, tt * nb, LANES), F32),
            pltpu.VMEM((tt * nb, nstate), F32),
            pltpu.VMEM((tt * nb, nstate), F32),
            pltpu.VMEM((width // LANES, tt * nb, LANES), F32),
            pltpu.VMEM((nb, nstate), F32),
            pltpu.VMEM((nb, nstate), F32),
        ],
        compiler_params=_cparams(("arbitrary",)),
        name="ssm",
    )(u_tm, bb, a, ct, dskip, gw, gb)


def _route(logits):
    lane = lax.broadcasted_iota(jnp.int32, logits.shape, 1)
    far = jnp.int32(4 * ROUTE_LANES)
    gl = jnp.where(lane < MOE_GROUPS, logits, NEG_BIG)
    gmax = jnp.max(gl, axis=1, keepdims=True)
    gidx = jnp.min(jnp.where(gl == gmax, lane, far), axis=1, keepdims=True)
    pg_top = 1.0 / jnp.sum(jnp.exp(gl - gmax), axis=1, keepdims=True)
    in_group = (lane >= MOE_GROUPS) & (((lane - MOE_GROUPS) // EXPERTS_PER_GROUP) == gidx) \
        & (lane < MOE_GROUPS + N_EXPERTS)
    el = jnp.where(in_group, logits, NEG_BIG)
    m1 = jnp.max(el, axis=1, keepdims=True)
    i1 = jnp.min(jnp.where(el == m1, lane, far), axis=1, keepdims=True)
    el2 = jnp.where(lane == i1, NEG_BIG, el)
    m2 = jnp.max(el2, axis=1, keepdims=True)
    i2 = jnp.min(jnp.where(el2 == m2, lane, far), axis=1, keepdims=True)
    e2 = jnp.exp(m2 - m1)
    w1 = pg_top / (1.0 + e2)
    w2 = pg_top * e2 / (1.0 + e2)
    idx = jnp.where(lane == 0, i1 - MOE_GROUPS, jnp.where(lane == 1, i2 - MOE_GROUPS, 0))
    wts = jnp.where(lane == 0, w1, jnp.where(lane == 1, w2, 0.0))
    return idx, wts


def _merge_kernel(x_ref, mod_ref, ua_ref, o_ref, yc_ref, g1_ref, gw_ref, gb_ref, pw_ref, ps_ref,
                  pa_ref, pb_ref, pc_ref, wo_ref, g2_ref, rw_ref, rb_ref,
                  x1_ref, h2_ref, ri_ref, rwt_ref, ubuf_ref, *, ts, d):
    ti = pl.program_id(1)
    x = x_ref[0]
    m = mod_ref[0]
    h = _rms(x, g1_ref[...]) * (1.0 + m[1:2]) + m[0:1]
    gates = _sigmoid(_dot(h.astype(BF16), gw_ref[...]) + gb_ref[...])

    @pl.when(ti == 0)
    def _():
        ubuf_ref[0:POOL_HIST, :] = jnp.zeros((POOL_HIST, ubuf_ref.shape[1]), F32)

    u = ua_ref[0]
    ubuf_ref[POOL_HIST:POOL_HIST + ts, :] = u
    t_abs = (ti * ts + lax.broadcasted_iota(jnp.int32, (ts, 1), 0) + 1).astype(F32)
    lane = lax.broadcasted_iota(jnp.int32, u.shape, 1)
    grp = lane // (u.shape[1] // len(POOL_WINDOWS))
    run = jnp.zeros_like(u)
    pooled = jnp.zeros_like(u)
    for j in range(POOL_HIST):
        run = run + ubuf_ref[pl.ds(POOL_HIST - j, ts), :]
        if (j + 1) in POOL_WINDOWS:
            g = POOL_WINDOWS.index(j + 1)
            val = run / jnp.minimum(t_abs, float(j + 1)) - u
            pooled = jnp.where(grp == g, val, pooled)
    ubuf_ref[0:POOL_HIST, :] = ubuf_ref[ts:ts + POOL_HIST, :]
    ya = _dot(pooled.astype(BF16), pw_ref[...]) * ps_ref[...]

    mixed = (gates[:, 0:d] * _dot(ya.astype(BF16), pa_ref[...])
             + gates[:, d:2 * d] * _dot(o_ref[0], pb_ref[...])
             + gates[:, 2 * d:3 * d] * _dot(yc_ref[...].astype(BF16), pc_ref[...]))
    x1 = x + m[2:3] * _dot(mixed.astype(BF16), wo_ref[...])
    x1_ref[0] = x1
    h2 = _rms(x1, g2_ref[...]) * (1.0 + m[4:5]) + m[3:4]
    for j in range(d // LANES):
        h2_ref[pl.ds(j, ts, stride=d // LANES), :] = h2[:, j * LANES:(j + 1) * LANES]
    logits = _dot(h2.astype(BF16), rw_ref[...]) + rb_ref[...]
    idx, wts = _route(logits)
    ri_ref[0] = idx
    rwt_ref[0] = wts


def _merge(x, mod, ua, o, yc_tm, g1, gw, gb, pw, ps, pa, pb, pc, wo, g2, rw, rb):
    nb, s, d = x.shape
    ts = min(SEQ_TILE, s)
    full = lambda arr: pl.BlockSpec(arr.shape, lambda b, t: (0,) * arr.ndim)
    tile = lambda w: pl.BlockSpec((1, ts, w), lambda b, t: (b, t, 0))
    return pl.pallas_call(
        functools.partial(_merge_kernel, ts=ts, d=d),
        grid=(nb, s // ts),
        in_specs=[tile(d), pl.BlockSpec((1, 6, d), lambda b, t: (b, 0, 0)), tile(256), tile(512),
                  pl.BlockSpec((ts, 256), lambda b, t: (t, b)),
                  full(g1), full(gw), full(gb), full(pw), full(ps), full(pa), full(pb), full(pc),
                  full(wo), full(g2), full(rw), full(rb)],
        out_specs=[tile(d), pl.BlockSpec((ts * (d // LANES), LANES), lambda b, t: (b * (s // ts) + t, 0)),
                   tile(ROUTE_LANES), tile(ROUTE_LANES)],
        out_shape=[
            jax.ShapeDtypeStruct((nb, s, d), F32),
            jax.ShapeDtypeStruct((nb * s * (d // LANES), LANES), F32),
            jax.ShapeDtypeStruct((nb, s, ROUTE_LANES), jnp.int32),
            jax.ShapeDtypeStruct((nb, s, ROUTE_LANES), F32),
        ],
        scratch_shapes=[pltpu.VMEM((POOL_HIST + ts, 256), F32)],
        compiler_params=_cparams(("arbitrary", "arbitrary")),
        name="merge_route",
    )(x, mod, ua, o, yc_tm, g1, gw, gb, pw, ps, pa, pb, pc, wo, g2, rw, rb)


def _expert_kernel(blk_e, first, nused, tok_ref, tokn_ref, h_hbm, w1_ref, w3_ref, w2_ref, y_ref,
                   xbuf, w1b, w3b, w2b, gsem, *, tmb, npc):
    i = pl.program_id(0)
    n = nused[0]
    slot = i % 2
    other = 1 - slot
    rows = tmb * npc

    def base(s):
        return pl.multiple_of(s * rows, rows)

    def gather_start(idx_ref, s, lo=0, hi=tmb):
        for r in range(lo, hi):
            pltpu.make_async_copy(h_hbm.at[pl.ds(pl.multiple_of(idx_ref[r], npc), npc), :],
                                  xbuf.at[pl.ds(base(s) + r * npc, npc), :], gsem.at[s]).start(priority=r % 2)

    def gather_wait(s):
        pltpu.make_async_copy(h_hbm.at[pl.ds(0, rows), :], xbuf.at[pl.ds(base(s), rows), :], gsem.at[s]).wait()

    @pl.when(i == 0)
    def _():
        gather_start(tok_ref, 0)

    @pl.when((i < n) & (first[i] == 1))
    def _():
        w1b[...] = w1_ref[0, 0].astype(BF16)
        w3b[...] = w3_ref[0, 0].astype(BF16)
        w2b[...] = w2_ref[0, 0].astype(BF16)

    @pl.when(i < n)
    def _():
        gather_wait(slot)
        xb = jnp.concatenate([xbuf[pl.ds(base(slot) + j, tmb, stride=npc), :] for j in range(npc)],
                             axis=1).astype(BF16)
        npieces = (2 * w1b.shape[1] + w2b.shape[1]) // PIECE
        per = tmb // npieces
        done = [0]

        def issue(p, n):
            gather_start(tokn_ref, other, done[0] * per, (done[0] + 1) * per)
            done[0] += 1

        a = _dot_pieces(xb, w1b, issue)
        g = _dot_pieces(xb, w3b, issue)
        mid = (a * _sigmoid(a)) * g
        y = _dot_pieces(mid.astype(BF16), w2b, issue)
        for j in range(npc):
            y_ref[pl.ds(j, tmb, stride=npc), :] = y[:, j * LANES:(j + 1) * LANES]

    @pl.when(i == n - 1)
    def _():
        gather_wait(other)

    @pl.when(i >= n)
    def _():
        y_ref[...] = jnp.zeros(y_ref.shape, F32)


def _experts(h2, tok, blk_e, first, nused, w1, w3, w2, layer):
    d = w1.shape[2]
    npc = d // LANES
    nslot = tok.shape[0]
    tmb = MOE_TILE
    nblk = nslot // tmb
    ff = w1.shape[3]
    smem_blk = lambda f: pl.BlockSpec((tmb,), f, memory_space=pltpu.SMEM)
    grid_spec = pltpu.PrefetchScalarGridSpec(
        num_scalar_prefetch=3,
        grid=(nblk,),
        in_specs=[
            smem_blk(lambda i, be, fi, nu: (i,)),
            smem_blk(lambda i, be, fi, nu: (jnp.minimum(i + 1, nblk - 1),)),
            pl.BlockSpec(memory_space=pl.ANY),
            pl.BlockSpec((1, 1, d, ff), lambda i, be, fi, nu: (layer, be[i], 0, 0)),
            pl.BlockSpec((1, 1, d, ff), lambda i, be, fi, nu: (layer, be[i], 0, 0)),
            pl.BlockSpec((1, 1, ff, d), lambda i, be, fi, nu: (layer, be[i], 0, 0)),
        ],
        out_specs=pl.BlockSpec((tmb * npc, LANES), lambda i, be, fi, nu: (i, 0)),
        scratch_shapes=[
            pltpu.VMEM((2 * tmb * npc, LANES), F32),
            pltpu.VMEM((d, ff), BF16),
            pltpu.VMEM((d, ff), BF16),
            pltpu.VMEM((ff, d), BF16),
            pltpu.SemaphoreType.DMA((2,)),
        ],
    )
    return pl.pallas_call(
        functools.partial(_expert_kernel, tmb=tmb, npc=npc),
        grid_spec=grid_spec,
        out_shape=jax.ShapeDtypeStruct((nslot * npc, LANES), F32),
        compiler_params=_cparams(("arbitrary",)),
        name="experts",
    )(blk_e, first, nused, tok, tok, h2, w1, w3, w2)


def _dispatch_plan(e_idx, tmb, row_scale):
    t = e_idx.shape[0]
    e_flat = e_idx.reshape(-1)
    onehot = (e_flat[:, None] == jnp.arange(N_EXPERTS, dtype=jnp.int32)[None, :]).astype(jnp.int32)
    csum = jnp.cumsum(onehot, axis=0)
    counts = csum[-1]
    padded = (counts + tmb - 1) // tmb * tmb
    pend = jnp.cumsum(padded)
    pstart = pend - padded
    dest = jnp.sum(onehot * (csum - 1 + pstart[None, :]), axis=1)
    nslot = 2 * t + N_EXPERTS * tmb
    src = jnp.full((nslot,), -1, jnp.int32).at[dest].set(jnp.arange(2 * t, dtype=jnp.int32))
    tok = jnp.maximum(src, 0) // 2
    blk_start = jnp.arange(nslot // tmb, dtype=jnp.int32) * tmb
    blk_e = jnp.minimum(jnp.sum(pend[None, :] <= blk_start[:, None], axis=1), N_EXPERTS - 1).astype(jnp.int32)
    first = jnp.concatenate([jnp.ones((1,), jnp.int32), (blk_e[1:] != blk_e[:-1]).astype(jnp.int32)])
    nused = (pend[-1] // tmb).astype(jnp.int32).reshape(1)
    dest2 = (dest * row_scale).reshape(t, 2)
    return tok * row_scale, dest2[:, 0], dest2[:, 1], blk_e, first, nused


def _block_diag(blocks):
    g, r, c = blocks.shape
    eye = jnp.eye(g, dtype=blocks.dtype)
    return jnp.einsum('grc,gh->grhc', blocks, eye).reshape(g * r, g * c)


def _ssm_params(lam_re, lam_im, log_dt, b_re, b_im, c_re, c_im):
    dt = jnp.exp(log_dt)[:, None]
    mag = jnp.exp(lam_re * dt)
    ar, ai = mag * jnp.cos(lam_im * dt), mag * jnp.sin(lam_im * dt)
    nr, ni = ar - 1.0, ai
    den = lam_re * lam_re + lam_im * lam_im
    kr, ki = (nr * lam_re + ni * lam_im) / den, (ni * lam_re - nr * lam_im) / den
    bbr = kr[..., None] * b_re - ki[..., None] * b_im
    bbi = kr[..., None] * b_im + ki[..., None] * b_re
    bb = jnp.concatenate([_block_diag(jnp.swapaxes(bbr, 1, 2)), _block_diag(jnp.swapaxes(bbi, 1, 2))], axis=1)
    ct = jnp.concatenate([_block_diag(jnp.swapaxes(c_re, 1, 2)), -_block_diag(jnp.swapaxes(c_im, 1, 2))], axis=0)
    a = jnp.stack([ar.reshape(-1), ai.reshape(-1)], axis=0)
    return bb.astype(BF16), a, ct.astype(BF16)


def _rope_tables(positions):
    inv = ROPE_THETA ** (-jnp.arange(0, HEAD_DIM, 2, dtype=F32) / HEAD_DIM)
    ang = positions.astype(F32)[..., None] * inv
    cos, sin = jnp.cos(ang), jnp.sin(ang)
    cosf = jnp.concatenate([cos, cos, cos, cos], axis=-1)
    sinf = jnp.concatenate([-sin, sin, -sin, sin], axis=-1)
    return cosf, sinf


def kernel(x, c, positions, ada_w, ada_b, norm1_g, w_in, pool_w, pool_scale, q_norm_g, k_norm_g, lambda_q1, lambda_k1, lambda_q2, lambda_k2, subln_g, lam_re, lam_im, log_dt, b_re, b_im, c_re, c_im, d_skip, glu_w, glu_b, proj_a, proj_b, proj_c, gate_w, gate_b, w_out, norm2_g, router_g_w, router_g_b, router_e_w, router_e_b, moe_w1, moe_w3, moe_w2):
    nb, s, d = x.shape
    depth = ada_w.shape[0]
    cosf, sinf = _rope_tables(positions)
    mod_all = _ada_mod(c, ada_w, ada_b).reshape(depth, nb, 6, d)
    ones = _block_diag(jnp.ones((256 // HEAD_DIM, HEAD_DIM, HEAD_DIM), F32)).astype(BF16)
    row = lambda v: v.reshape(1, -1)
    tile2 = lambda v: jnp.concatenate([v, v]).reshape(1, -1)
    moe = None
    for l in range(depth):
        lambda_init = 0.8 - 0.6 * math.exp(-0.3 * l)
        mod = mod_all[l]
        x, ua, q, k, v, uc_tm = _inproj(x, mod, row(norm1_g[l]), w_in[l].astype(BF16),
                                        tile2(q_norm_g[l]), tile2(k_norm_g[l]), ones, cosf, sinf, moe)
        lam = (jnp.exp(jnp.sum(lambda_q1[l] * lambda_k1[l])) - jnp.exp(jnp.sum(lambda_q2[l] * lambda_k2[l]))
               + lambda_init).astype(F32)
        score_bound = 1.02 * LOG2E * math.sqrt(HEAD_DIM) * jnp.max(jnp.abs(q_norm_g[l])) * jnp.max(jnp.abs(k_norm_g[l]))
        par = jnp.stack([lam, (score_bound <= SHIFT_FREE_LIMIT).astype(F32)])
        o = _attention(q, k, v, par, row(subln_g[l]), lambda_init)
        bb, a, ct = _ssm_params(lam_re[l], lam_im[l], log_dt[l], b_re[l], b_im[l], c_re[l], c_im[l])
        yc_tm = _ssm(uc_tm, bb, a, ct, row(d_skip[l]), glu_w[l].astype(BF16), row(glu_b[l]), nb)
        rw = jnp.zeros((d, ROUTE_LANES), F32).at[:, 0:MOE_GROUPS].set(router_g_w[l]) \
            .at[:, MOE_GROUPS:MOE_GROUPS + N_EXPERTS].set(router_e_w[l]).astype(BF16)
        rb = jnp.zeros((1, ROUTE_LANES), F32).at[0, 0:MOE_GROUPS].set(router_g_b[l]) \
            .at[0, MOE_GROUPS:MOE_GROUPS + N_EXPERTS].set(router_e_b[l])
        x1, h2, ri, rwt = _merge(x, mod, ua, o, yc_tm, row(norm1_g[l]), gate_w[l].astype(BF16), row(gate_b[l]),
                                 _block_diag(pool_w[l]).astype(BF16), row(pool_scale[l]),
                                 proj_a[l].astype(BF16), proj_b[l].astype(BF16), proj_c[l].astype(BF16),
                                 w_out[l].astype(BF16), row(norm2_g[l]), rw, rb)
        t = nb * s
        tok, d0, d1, blk_e, first, nused = _dispatch_plan(ri.reshape(t, ROUTE_LANES)[:, 0:2], MOE_TILE, d // LANES)
        y = _experts(h2, tok, blk_e, first, nused, moe_w1, moe_w3, moe_w2, l)
        moe = (d0, d1, x1, mod, rwt, y)
    return _moe_combine(moe, nb, s, d)
```
